```python
import math, functools
import jax, jax.numpy as jnp
from jax import lax
import numpy as np

D_MODEL = 1024
BATCH = 4
SEQ = 4096
DEPTH = 1
DEC_BATCH = 128
DEC_SEQ = 4
PAST_LEN = 16384
PAGE_SIZE = 128

MLA_HEADS = 16
QK_NOPE = 64
QK_ROPE = 32
V_HEAD = 64
Q_LORA = 384
KV_LORA = 256
ROPE_THETA = 10000.0
Q_BLOCK = 128
ATTN_SCALE = (QK_NOPE + QK_ROPE) ** -0.5
SSM_EXPAND = 2
D_INNER = SSM_EXPAND * D_MODEL
SSM_HEADDIM = 64
SSM_HEADS = D_INNER // SSM_HEADDIM
SSM_GROUPS = 4
SSM_STATE = 128
CONV_WIDTH = 4
CONV_DIM = D_INNER + 2 * SSM_GROUPS * SSM_STATE
SSD_CHUNK = 256
D_FF = ((8 * D_MODEL // 3 + 255) // 256) * 256
ALPHA = (2.0 * DEPTH) ** 0.25
BETA = (8.0 * DEPTH) ** -0.25
IN_SPLITS = (Q_LORA, KV_LORA, QK_ROPE, D_INNER, CONV_DIM, SSM_HEADS, 2 * D_MODEL)
IN_OFFSETS = tuple(sum(IN_SPLITS[:i + 1]) for i in range(len(IN_SPLITS) - 1))
D_IN_PROJ = sum(IN_SPLITS)

kernel_name = "hybrid_mla_ssd_gated_deepnorm_step"


def rms_norm(x, g, eps):
    xf = x.astype(jnp.float32)
    xf = xf * lax.rsqrt(jnp.mean(xf * xf, axis=-1, keepdims=True) + eps)
    return (xf * g.astype(jnp.float32)).astype(x.dtype)


def layer_norm(x, g, b, eps=1e-5):
    xf = x.astype(jnp.float32)
    mu = jnp.mean(xf, axis=-1, keepdims=True)
    var = jnp.mean(jnp.square(xf - mu), axis=-1, keepdims=True)
    out = (xf - mu) * lax.rsqrt(var + eps) * g.astype(jnp.float32) + b.astype(jnp.float32)
    return out.astype(x.dtype)


def apply_rope(x, pos):
    inv = ROPE_THETA ** (-jnp.arange(0, QK_ROPE, 2, dtype=jnp.float32) / QK_ROPE)
    ang = pos.astype(jnp.float32)[:, None] * inv[None, :]
    cos = jnp.cos(ang)[None, :, None, :]
    sin = jnp.sin(ang)[None, :, None, :]
    xf = x.astype(jnp.float32)
    x1, x2 = xf[..., :QK_ROPE // 2], xf[..., QK_ROPE // 2:]
    out = jnp.concatenate([x1 * cos - x2 * sin, x2 * cos + x1 * sin], axis=-1)
    return out.astype(x.dtype)


def mla_project(c_q, c_kv, k_pe, pos, q_norm_g, kv_norm_g, w_uq):
    B, T, _ = c_q.shape
    q = jnp.einsum('btr,re->bte', rms_norm(c_q, q_norm_g, 1e-6), w_uq)
    q = q.reshape(B, T, MLA_HEADS, QK_NOPE + QK_ROPE)
    q_nope = q[..., :QK_NOPE]
    q_pe = apply_rope(q[..., QK_NOPE:], pos)
    c_kv = rms_norm(c_kv, kv_norm_g, 1e-6)
    k_pe = apply_rope(k_pe[:, :, None, :], pos)[:, :, 0, :]
    return q_nope, q_pe, c_kv, k_pe


def mla_prompt_attention(q_nope, q_pe, c_kv, k_pe, w_ukv):
    B, S, _, _ = q_nope.shape
    kv = jnp.einsum('bsc,ce->bse', c_kv, w_ukv).reshape(B, S, MLA_HEADS, QK_NOPE + V_HEAD)
    k_nope, v = kv[..., :QK_NOPE], kv[..., QK_NOPE:]
    key_pos = jnp.arange(S)

    def block(i):
        start = i * Q_BLOCK
        qn = lax.dynamic_slice_in_dim(q_nope, start, Q_BLOCK, axis=1)
        qp = lax.dynamic_slice_in_dim(q_pe, start, Q_BLOCK, axis=1)
        s = (jnp.einsum('bqhd,bkhd->bhqk', qn, k_nope)
             + jnp.einsum('bqhr,bkr->bhqk', qp, k_pe)).astype(jnp.float32) * ATTN_SCALE
        q_pos = start + jnp.arange(Q_BLOCK)
        mask = key_pos[None, :] <= q_pos[:, None]
        s = jnp.where(mask, s, jnp.finfo(jnp.float32).min)
        p = jax.nn.softmax(s, axis=-1).astype(v.dtype)
        return jnp.einsum('bhqk,bkhd->bqhd', p, v)

    out = lax.map(block, jnp.arange(S // Q_BLOCK))
    return out.transpose(1, 0, 2, 3, 4).reshape(B, S, MLA_HEADS * V_HEAD)


def mla_sample_attention(q_nope, q_pe, c_kv_new, k_pe_new, cache_ckv, cache_kpe, layer, page_table, w_ukv):
    B, T, _, _ = q_nope.shape
    w = w_ukv.reshape(KV_LORA, MLA_HEADS, QK_NOPE + V_HEAD)
    w_uk, w_uv = w[..., :QK_NOPE], w[..., QK_NOPE:]
    past_ckv = cache_ckv[layer, page_table].reshape(B, -1, KV_LORA)
    past_kpe = cache_kpe[layer, page_table].reshape(B, -1, QK_ROPE)
    past_len = past_ckv.shape[1]
    q_lat = jnp.einsum('bthd,chd->bthc', q_nope, w_uk)
    s_past = (jnp.einsum('bthc,bkc->bhtk', q_lat, past_ckv)
              + jnp.einsum('bthr,bkr->bhtk', q_pe, past_kpe)).astype(jnp.float32)
    s_new = (jnp.einsum('bthc,bsc->bhts', q_lat, c_kv_new)
             + jnp.einsum('bthr,bsr->bhts', q_pe, k_pe_new)).astype(jnp.float32)
    causal = jnp.arange(T)[None, :] <= jnp.arange(T)[:, None]
    s_new = jnp.where(causal, s_new, jnp.finfo(jnp.float32).min)
    s = jnp.concatenate([s_past, s_new], axis=-1) * ATTN_SCALE
    p = jax.nn.softmax(s, axis=-1).astype(c_kv_new.dtype)
    o_lat = (jnp.einsum('bhtk,bkc->bthc', p[..., :past_len], past_ckv)
             + jnp.einsum('bhts,bsc->bthc', p[..., past_len:], c_kv_new))
    o = jnp.einsum('bthc,chd->bthd', o_lat, w_uv)
    return o.reshape(B, T, MLA_HEADS * V_HEAD)


def causal_conv(xbc, conv_state, conv_w, conv_b):
    T = xbc.shape[1]
    xpad = jnp.concatenate([conv_state.astype(xbc.dtype), xbc], axis=1)
    out = conv_b + sum(xpad[:, k:k + T] * conv_w[k] for k in range(CONV_WIDTH))
    return jax.nn.silu(out), xpad[:, xpad.shape[1] - (CONV_WIDTH - 1):]


def ssd_scan(x, dt, a, b, c, init_state):
    Bsz, L = x.shape[0], x.shape[1]
    chunk = min(SSD_CHUNK, L)
    pad = (-L) % chunk
    if pad:
        pw = lambda t: jnp.pad(t, [(0, 0), (0, pad)] + [(0, 0)] * (t.ndim - 2))
        x, dt, b, c = pw(x), pw(dt), pw(b), pw(c)
    nc = (L + pad) // chunk
    G, Hg = SSM_GROUPS, SSM_HEADS // SSM_GROUPS
    xr = x.reshape(Bsz, nc, chunk, G, Hg, SSM_HEADDIM)
    dtr = dt.reshape(Bsz, nc, chunk, G, Hg)
    br = b.reshape(Bsz, nc, chunk, G, SSM_STATE)
    cr = c.reshape(Bsz, nc, chunk, G, SSM_STATE)
    a_cum = jnp.cumsum(dtr * a.reshape(G, Hg), axis=2)
    xdt = xr * dtr[..., None]
    seg = a_cum[:, :, :, None] - a_cum[:, :, None, :]
    tril = (jnp.arange(chunk)[:, None] >= jnp.arange(chunk)[None, :])[:, :, None, None]
    decay = jnp.exp(jnp.where(tril, seg, -jnp.inf))
    cb = jnp.einsum('bclgn,bcsgn->bclsg', cr, br)
    y_diag = jnp.einsum('bclsg,bclsgh,bcsghp->bclghp', cb, decay, xdt)
    decay_to_end = jnp.exp(a_cum[:, :, -1:] - a_cum)
    chunk_states = jnp.einsum('bclgn,bclgh,bclghp->bcghpn', br, decay_to_end, xdt)
    chunk_decay = jnp.exp(a_cum[:, :, -1])

    def step(state, inp):
        cs, cd = inp
        return state * cd[..., None, None] + cs, state

    init = init_state.astype(jnp.float32).reshape(Bsz, G, Hg, SSM_HEADDIM, SSM_STATE)
    final, prev = lax.scan(step, init, (jnp.swapaxes(chunk_states, 0, 1).astype(jnp.float32),
                                        jnp.swapaxes(chunk_decay, 0, 1)))
    prev = jnp.swapaxes(prev, 0, 1)
    y_off = jnp.einsum('bclgn,bcghpn,bclgh->bclghp', cr, prev, jnp.exp(a_cum))
    y = (y_diag + y_off).reshape(Bsz, nc * chunk, SSM_HEADS, SSM_HEADDIM)[:, :L]
    return y.astype(x.dtype), final.reshape(Bsz, SSM_HEADS, SSM_HEADDIM, SSM_STATE).astype(init_state.dtype)


def mamba2_branch(z, xbc, dt_raw, conv_state, ssm_state, conv_w, conv_b, dt_bias, a_log, d_skip, ssm_norm_g):
    B, T, _ = z.shape
    xbc_c, new_conv = causal_conv(xbc, conv_state, conv_w, conv_b)
    xs = xbc_c[..., :D_INNER].reshape(B, T, SSM_HEADS, SSM_HEADDIM)
    bs = xbc_c[..., D_INNER:D_INNER + SSM_GROUPS * SSM_STATE].reshape(B, T, SSM_GROUPS, SSM_STATE)
    cs = xbc_c[..., D_INNER + SSM_GROUPS * SSM_STATE:].reshape(B, T, SSM_GROUPS, SSM_STATE)
    dt = jax.nn.softplus(dt_raw.astype(jnp.float32) + dt_bias.astype(jnp.float32))
    a = -jnp.exp(a_log.astype(jnp.float32))
    y, new_ssm = ssd_scan(xs, dt, a, bs, cs, ssm_state)
    y = (y + xs * d_skip[:, None]).reshape(B, T, D_INNER)
    yg = (y * jax.nn.silu(z)).reshape(B, T, SSM_GROUPS, D_INNER // SSM_GROUPS)
    yg = rms_norm(yg, ssm_norm_g.reshape(SSM_GROUPS, D_INNER // SSM_GROUPS), 1e-5)
    return yg.reshape(B, T, D_INNER), new_conv, new_ssm


def swiglu_ffn(h, w_ffn_gate, w_ffn_up, w_ffn_down):
    g = jnp.einsum('btd,df->btf', h, w_ffn_gate)
    u = jnp.einsum('btd,df->btf', h, w_ffn_up)
    return jnp.einsum('btf,fd->btd', jax.nn.silu(g) * u, w_ffn_down)


def hybrid_layer(x, pos, conv_state, ssm_state, attend, w_in, b_gate, q_norm_g, kv_norm_g, w_uq,
                 conv_w, conv_b, dt_bias, a_log, d_skip, ssm_norm_g, w_branch_a, w_branch_b, w_out,
                 ln1_g, ln1_b, w_ffn_gate, w_ffn_up, w_ffn_down, ln2_g, ln2_b):
    proj = jnp.einsum('btd,de->bte', x, w_in)
    c_q, c_kv, k_pe, z, xbc, dt_raw, g_logits = jnp.split(proj, IN_OFFSETS, axis=-1)
    q_nope, q_pe, c_kv, k_pe = mla_project(c_q, c_kv, k_pe, pos, q_norm_g, kv_norm_g, w_uq)
    attn = attend(q_nope, q_pe, c_kv, k_pe)
    ssm_out, new_conv, new_ssm = mamba2_branch(z, xbc, dt_raw, conv_state, ssm_state, conv_w, conv_b,
                                               dt_bias, a_log, d_skip, ssm_norm_g)
    gates = jax.nn.sigmoid(g_logits + b_gate)
    g_a, g_b = gates[..., :D_MODEL], gates[..., D_MODEL:]
    merged = (g_a * jnp.einsum('bte,ed->btd', attn, w_branch_a)
              + g_b * jnp.einsum('bte,ed->btd', ssm_out, w_branch_b))
    mix = jnp.einsum('btd,de->bte', merged, w_out)
    h = layer_norm(ALPHA * x + mix, ln1_g, ln1_b)
    y = layer_norm(ALPHA * h + swiglu_ffn(h, w_ffn_gate, w_ffn_up, w_ffn_down), ln2_g, ln2_b)
    return y, c_kv, k_pe, new_conv, new_ssm


def setup_inputs(seed: int = 0) -> dict:
    key = jax.random.key(seed)
    ks = jax.random.split(key, 40)
    f32 = jnp.float32
    n_pages = PAST_LEN // PAGE_SIZE
    n_used = DEC_BATCH * n_pages
    n_pool = -(-5 * n_used // 4)
    nrm = lambda k, shape, s: jax.random.normal(k, shape, f32) * s
    dt0 = jnp.exp(jax.random.uniform(ks[12], (DEPTH, SSM_HEADS), f32, math.log(1e-3), math.log(1e-1)))
    page_table = jax.random.permutation(ks[6], n_pool)[:n_used].reshape(DEC_BATCH, n_pages).astype(jnp.int32)
    return {
        "x_prompt": nrm(ks[0], (BATCH, SEQ, D_MODEL), 1.0),
        "x_sample": nrm(ks[1], (DEC_BATCH, DEC_SEQ, D_MODEL), 1.0),
        "cache_ckv": nrm(ks[2], (DEPTH, n_pool, PAGE_SIZE, KV_LORA), 1.0),
        "cache_kpe": nrm(ks[3], (DEPTH, n_pool, PAGE_SIZE, QK_ROPE), 1.0),
        "state_ssm": nrm(ks[4], (DEPTH, DEC_BATCH, SSM_HEADS, SSM_HEADDIM, SSM_STATE), 0.1),
        "state_conv": nrm(ks[5], (DEPTH, DEC_BATCH, CONV_WIDTH - 1, CONV_DIM), 1.0),
        "page_table": page_table,
        "w_in": nrm(ks[7], (DEPTH, D_MODEL, D_IN_PROJ), D_MODEL ** -0.5),
        "b_gate": nrm(ks[8], (DEPTH, 2 * D_MODEL), 0.1),
        "q_norm_g": 1.0 + nrm(ks[9], (DEPTH, Q_LORA), 0.02),
        "kv_norm_g": 1.0 + nrm(ks[10], (DEPTH, KV_LORA), 0.02),
        "w_uq": nrm(ks[11], (DEPTH, Q_LORA, MLA_HEADS * (QK_NOPE + QK_ROPE)), Q_LORA ** -0.5),
        "w_ukv": nrm(ks[13], (DEPTH, KV_LORA, MLA_HEADS * (QK_NOPE + V_HEAD)), KV_LORA ** -0.5),
        "conv_w": nrm(ks[14], (DEPTH, CONV_WIDTH, CONV_DIM), CONV_WIDTH ** -0.5),
        "conv_b": nrm(ks[15], (DEPTH, CONV_DIM), 0.02),
        "dt_bias": dt0 + jnp.log(-jnp.expm1(-dt0)),
        "a_log": jnp.log(jax.random.uniform(ks[16], (DEPTH, SSM_HEADS), f32, 1.0, 16.0)),
        "d_skip": 1.0 + nrm(ks[17], (DEPTH, SSM_HEADS), 0.1),
        "ssm_norm_g": 1.0 + nrm(ks[18], (DEPTH, D_INNER), 0.02),
        "w_branch_a": nrm(ks[19], (DEPTH, MLA_HEADS * V_HEAD, D_MODEL), (MLA_HEADS * V_HEAD) ** -0.5),
        "w_branch_b": nrm(ks[20], (DEPTH, D_INNER, D_MODEL), D_INNER ** -0.5),
        "w_out": nrm(ks[21], (DEPTH, D_MODEL, D_MODEL), BETA * D_MODEL ** -0.5),
        "ln1_g": 1.0 + nrm(ks[22], (DEPTH, D_MODEL), 0.02),
        "ln1_b": nrm(ks[23], (DEPTH, D_MODEL), 0.02),
        "w_ffn_gate": nrm(ks[24], (DEPTH, D_MODEL, D_FF), D_MODEL ** -0.5),
        "w_ffn_up": nrm(ks[25], (DEPTH, D_MODEL, D_FF), D_MODEL ** -0.5),
        "w_ffn_down": nrm(ks[26], (DEPTH, D_FF, D_MODEL), BETA * D_FF ** -0.5),
        "ln2_g": 1.0 + nrm(ks[27], (DEPTH, D_MODEL), 0.02),
        "ln2_b": nrm(ks[28], (DEPTH, D_MODEL), 0.02),
    }


def reference(x_prompt, x_sample, cache_ckv, cache_kpe, state_ssm, state_conv, page_table,
              w_in, b_gate, q_norm_g, kv_norm_g, w_uq, w_ukv, conv_w, conv_b, dt_bias, a_log, d_skip,
              ssm_norm_g, w_branch_a, w_branch_b, w_out, ln1_g, ln1_b, w_ffn_gate, w_ffn_up, w_ffn_down,
              ln2_g, ln2_b):
    B, S, _ = x_prompt.shape
    T = x_sample.shape[1]
    past_len = page_table.shape[1] * PAGE_SIZE
    pos_p = jnp.arange(S)
    pos_s = past_len + jnp.arange(T)
    xp, xs = x_prompt, x_sample
    ckv_p, kpe_p, ssm_p, conv_p = [], [], [], []
    ckv_s, kpe_s, ssm_s, conv_s = [], [], [], []
    for l in range(DEPTH):
        lw = (w_in[l], b_gate[l], q_norm_g[l], kv_norm_g[l], w_uq[l], conv_w[l], conv_b[l], dt_bias[l],
              a_log[l], d_skip[l], ssm_norm_g[l], w_branch_a[l], w_branch_b[l], w_out[l], ln1_g[l], ln1_b[l],
              w_ffn_gate[l], w_ffn_up[l], w_ffn_down[l], ln2_g[l], ln2_b[l])
        conv0 = jnp.zeros((B, CONV_WIDTH - 1, CONV_DIM), xp.dtype)
        ssm0 = jnp.zeros((B, SSM_HEADS, SSM_HEADDIM, SSM_STATE), xp.dtype)
        attend_p = functools.partial(mla_prompt_attention, w_ukv=w_ukv[l])
        attend_s = functools.partial(mla_sample_attention, cache_ckv=cache_ckv, cache_kpe=cache_kpe,
                                     layer=l, page_table=page_table, w_ukv=w_ukv[l])
        xp, a1, a2, a3, a4 = hybrid_layer(xp, pos_p, conv0, ssm0, attend_p, *lw)
        xs, b1, b2, b3, b4 = hybrid_layer(xs, pos_s, state_conv[l], state_ssm[l], attend_s, *lw)
        ckv_p.append(a1); kpe_p.append(a2); conv_p.append(a3); ssm_p.append(a4)
        ckv_s.append(b1); kpe_s.append(b2); conv_s.append(b3); ssm_s.append(b4)
    return (xp, xs, jnp.stack(ckv_p), jnp.stack(kpe_p), jnp.stack(ssm_p), jnp.stack(conv_p),
            jnp.stack(ckv_s), jnp.stack(kpe_s), jnp.stack(ssm_s), jnp.stack(conv_s))
```

```python
import functools
import math

import jax
import jax.numpy as jnp
from jax import lax
from jax.experimental import pallas as pl
from jax.experimental.pallas import tpu as pltpu

F32 = jnp.float32
BF16 = jnp.bfloat16

D_MODEL = 1024
MLA_HEADS = 16
QK_NOPE = 64
QK_ROPE = 32
V_HEAD = 64
Q_LORA = 384
KV_LORA = 256
ROPE_THETA = 10000.0
ATTN_SCALE = (QK_NOPE + QK_ROPE) ** -0.5
D_INNER = 2 * D_MODEL
SSM_HEADDIM = 64
SSM_HEADS = D_INNER // SSM_HEADDIM
SSM_GROUPS = 4
HEADS_PER_GROUP = SSM_HEADS // SSM_GROUPS
SSM_STATE = 128
CONV_WIDTH = 4
CONV_DIM = D_INNER + 2 * SSM_GROUPS * SSM_STATE
SSD_CHUNK = 256
D_FF = ((8 * D_MODEL // 3 + 255) // 256) * 256
PAGE_SIZE = 128
DEPTH = 1
ALPHA = (2.0 * DEPTH) ** 0.25
IN_SPLITS = (Q_LORA, KV_LORA, QK_ROPE, D_INNER, CONV_DIM, SSM_HEADS, 2 * D_MODEL)
IN_OFFSETS = tuple(sum(IN_SPLITS[:i + 1]) for i in range(len(IN_SPLITS) - 1))

LANES = 128
SUBLANES = 8
VMEM_LIMIT_BYTES = 56 * 1024 * 1024

PROJ_XBC = 0
PROJ_SMALL = CONV_DIM
PROJ_Z = 4096
PROJ_G = 6144
PROJ_COLS = 8192
SMALL_CQ = 0
SMALL_CKV = Q_LORA
SMALL_KPE = Q_LORA + KV_LORA
SMALL_KROT = SMALL_KPE + LANES
SMALL_DT = SMALL_KROT + LANES
SMALL_COLS = 1024
PE_LANE0 = QK_NOPE


def _cparams(*sem):
    return pltpu.CompilerParams(dimension_semantics=sem, vmem_limit_bytes=VMEM_LIMIT_BYTES)


def _nt_dot(a, b):
    return lax.dot_general(a, b, (((1,), (1,)), ((), ())), preferred_element_type=F32)


def _silu(x):
    return x * (1.0 / (1.0 + jnp.exp(-x)))


def _sigmoid(x):
    return 1.0 / (1.0 + jnp.exp(-x))


def _softplus(x):
    return jnp.maximum(x, 0.0) + jnp.log(1.0 + jnp.exp(-jnp.abs(x)))


def _matmul_kernel(x_ref, w_ref, o_ref):
    o_ref[...] = jnp.dot(x_ref[...].astype(BF16), w_ref[...],
                         preferred_element_type=F32).astype(o_ref.dtype)


def _matmul(x, w, tm, tn, out_dtype):
    m, k = x.shape
    n = w.shape[1]
    return pl.pallas_call(
        _matmul_kernel,
        grid=(m // tm, n // tn),
        in_specs=[pl.BlockSpec((tm, k), lambda i, j: (i, 0)),
                  pl.BlockSpec((k, tn), lambda i, j: (0, j))],
        out_specs=pl.BlockSpec((tm, tn), lambda i, j: (i, j)),
        out_shape=jax.ShapeDtypeStruct((m, n), out_dtype),
        compiler_params=_cparams("parallel", "arbitrary"),
        name="in_proj",
    )(x, w)


def _rms(x, g, eps):
    return x * lax.rsqrt(jnp.mean(x * x, axis=-1, keepdims=True) + eps) * g


def _mla_common(blk, cos, sin, gq_ref, gkv_ref, ckv_ref, kpe_ref):
    cqn = _rms(blk[:, SMALL_CQ:SMALL_CQ + Q_LORA], gq_ref[...], 1e-6).astype(BF16)
    ckvn = _rms(blk[:, SMALL_CKV:SMALL_CKV + KV_LORA], gkv_ref[...], 1e-6)
    ckv_ref[...] = ckvn
    kr = blk[:, SMALL_KPE:SMALL_KPE + LANES] * cos + blk[:, SMALL_KROT:SMALL_KROT + LANES] * sin
    kpe_ref[...] = kr[:, PE_LANE0:PE_LANE0 + QK_ROPE]
    return cqn, ckvn, kr


def _mla_prep_prompt_kernel(blk_ref, cos_ref, sin_ref, gq_ref, gkv_ref, wq_ref, wkv_ref,
                            q_ref, k_ref, v_ref, ckv_ref, kpe_ref):
    cos = cos_ref[...]
    sin = sin_ref[...]
    cqn, ckvn, kr = _mla_common(blk_ref[...], cos, sin, gq_ref, gkv_ref, ckv_ref, kpe_ref)
    hw = MLA_HEADS * LANES
    qab = jnp.dot(cqn, wq_ref[...], preferred_element_type=F32)
    kv = jnp.dot(ckvn.astype(BF16), wkv_ref[...], preferred_element_type=F32)
    for h in range(MLA_HEADS):
        sl = slice(h * LANES, (h + 1) * LANES)
        q_ref[:, sl] = (qab[:, sl] * cos + qab[:, hw + h * LANES:hw + (h + 1) * LANES] * sin).astype(BF16)
        k_ref[:, sl] = (kv[:, sl] + kr).astype(BF16)
    v_ref[...] = kv[:, hw:].astype(BF16)


def _mla_prep_sample_kernel(blk_ref, cos_ref, sin_ref, cosp_ref, sinp_ref, gq_ref, gkv_ref,
                            wq_ref, wuk_ref, qlat_ref, qpe_ref, ckv_ref, kpe_ref):
    cqn, _, _ = _mla_common(blk_ref[...], cos_ref[...], sin_ref[...], gq_ref, gkv_ref, ckv_ref, kpe_ref)
    hw = MLA_HEADS * LANES
    pw = MLA_HEADS * QK_ROPE
    qall = jnp.dot(cqn, wq_ref[...], preferred_element_type=F32)
    qpe_ref[...] = (qall[:, hw:hw + pw] * cosp_ref[...] + qall[:, hw + pw:] * sinp_ref[...]).astype(BF16)
    for h in range(MLA_HEADS):
        qn = qall[:, h * LANES:(h + 1) * LANES].astype(BF16)
        qlat_ref[:, h * KV_LORA:(h + 1) * KV_LORA] = jnp.dot(
            qn, wuk_ref[h], preferred_element_type=F32).astype(BF16)


def _const_spec(shape):
    nd = len(shape)
    return pl.BlockSpec(shape, lambda *_: (0,) * nd)


def _mla_prep_prompt(proj, cos, sin, gq, gkv, wq, wkv, tm):
    n = proj.shape[0]
    hw = MLA_HEADS * LANES
    small_blk = PROJ_SMALL // SMALL_COLS
    row = lambda w: pl.BlockSpec((tm, w), lambda i: (i, 0))
    period = cos.shape[0] // tm
    tab = lambda w: pl.BlockSpec((tm, w), lambda i: (i % period, 0))
    return pl.pallas_call(
        _mla_prep_prompt_kernel,
        grid=(n // tm,),
        in_specs=[pl.BlockSpec((tm, SMALL_COLS), lambda i: (i, small_blk)),
                  tab(LANES), tab(LANES),
                  _const_spec(gq.shape), _const_spec(gkv.shape),
                  _const_spec(wq.shape), _const_spec(wkv.shape)],
        out_specs=[row(hw), row(hw), row(MLA_HEADS * V_HEAD), row(KV_LORA), row(QK_ROPE)],
        out_shape=[jax.ShapeDtypeStruct((n, hw), BF16), jax.ShapeDtypeStruct((n, hw), BF16),
                   jax.ShapeDtypeStruct((n, MLA_HEADS * V_HEAD), BF16),
                   jax.ShapeDtypeStruct((n, KV_LORA), F32), jax.ShapeDtypeStruct((n, QK_ROPE), F32)],
        compiler_params=_cparams("parallel"),
        name="mla_prep_prompt",
    )(proj, cos, sin, gq, gkv, wq, wkv)


def _mla_prep_sample(proj, cos, sin, cosp, sinp, gq, gkv, wq, wuk, tm):
    n = proj.shape[0]
    pw = MLA_HEADS * QK_ROPE
    small_blk = PROJ_SMALL // SMALL_COLS
    row = lambda w: pl.BlockSpec((tm, w), lambda i: (i, 0))
    tab = lambda w: pl.BlockSpec((tm, w), lambda i: (0, 0))
    return pl.pallas_call(
        _mla_prep_sample_kernel,
        grid=(n // tm,),
        in_specs=[pl.BlockSpec((tm, SMALL_COLS), lambda i: (i, small_blk)),
                  tab(LANES), tab(LANES), tab(pw), tab(pw),
                  _const_spec(gq.shape), _const_spec(gkv.shape),
                  _const_spec(wq.shape), _const_spec(wuk.shape)],
        out_specs=[row(MLA_HEADS * KV_LORA), row(pw), row(KV_LORA), row(QK_ROPE)],
        out_shape=[jax.ShapeDtypeStruct((n, MLA_HEADS * KV_LORA), BF16),
                   jax.ShapeDtypeStruct((n, pw), BF16),
                   jax.ShapeDtypeStruct((n, KV_LORA), F32), jax.ShapeDtypeStruct((n, QK_ROPE), F32)],
        compiler_params=_cparams("parallel"),
        name="mla_prep_sample",
    )(proj, cos, sin, cosp, sinp, gq, gkv, wq, wuk)


def _flash_kernel(q_ref, k_ref, v_ref, o_ref, *, tq):
    qi = pl.program_id(2)
    q2 = q_ref[...]
    row = lax.broadcasted_iota(jnp.int32, (tq, tq), 0)
    col = lax.broadcasted_iota(jnp.int32, (tq, tq), 1)
    causal = col <= row
    neg = jnp.finfo(F32).min
    outs = []
    for hh in range(2):
        q = q2[:, hh * LANES:(hh + 1) * LANES]

        def tile(j, carry, masked, hh=hh, q=q):
            m, l, acc = carry
            start = pl.multiple_of(j * tq, tq)
            k = k_ref[pl.ds(start, tq), hh * LANES:(hh + 1) * LANES]
            v = v_ref[pl.ds(start, tq), :]
            s = _nt_dot(q, k) * ATTN_SCALE
            if masked:
                s = jnp.where(causal, s, neg)
            m_new = jnp.maximum(m, jnp.max(s, axis=-1, keepdims=True))
            alpha = jnp.exp(m - m_new)
            p = jnp.exp(s - m_new)
            l = alpha * l + jnp.sum(p, axis=-1, keepdims=True)
            acc = alpha * acc + jnp.dot(p.astype(BF16), v, preferred_element_type=F32)
            return m_new, l, acc

        init = (jnp.full((tq, 1), -jnp.inf, F32), jnp.zeros((tq, 1), F32), jnp.zeros((tq, LANES), F32))
        carry = lax.fori_loop(0, qi, functools.partial(tile, masked=False), init)
        m, l, acc = tile(qi, carry, True)
        outs.append(acc / l)
    lane = lax.broadcasted_iota(jnp.int32, (tq, LANES), 1)
    o_ref[...] = jnp.where(lane < V_HEAD, outs[0], outs[1]).astype(o_ref.dtype)


def _prompt_attention(q, k, v, batch, seq, tq):
    n = batch * seq
    nq = seq // tq
    pairs = MLA_HEADS // 2
    return pl.pallas_call(
        functools.partial(_flash_kernel, tq=tq),
        grid=(batch, pairs, nq),
        in_specs=[pl.BlockSpec((tq, 2 * LANES), lambda b, p, i: (b * nq + i, p)),
                  pl.BlockSpec((seq, 2 * LANES), lambda b, p, i: (b, p)),
                  pl.BlockSpec((seq, 2 * V_HEAD), lambda b, p, i: (b, p))],
        out_specs=pl.BlockSpec((tq, 2 * V_HEAD), lambda b, p, i: (b * nq + i, p)),
        out_shape=jax.ShapeDtypeStruct((n, MLA_HEADS * V_HEAD), BF16),
        compiler_params=_cparams("parallel", "parallel", "arbitrary"),
        name="prompt_attention",
    )(q, k, v)


def _paged_attn_kernel(pt_ref, qlat_ref, qpe_ref, ckvn_ref, kpen_ref, *refs, pps, t_new):
    ckv_refs = refs[:pps]
    kpe_refs = refs[pps:2 * pps]
    o_ref = refs[2 * pps]
    m_sc, l_sc, acc_sc = refs[2 * pps + 1:]
    c = pl.program_id(1)
    rows = t_new * MLA_HEADS

    @pl.when(c == 0)
    def _():
        m_sc[...] = jnp.full(m_sc.shape, -jnp.inf, F32)
        l_sc[...] = jnp.zeros(l_sc.shape, F32)
        acc_sc[...] = jnp.zeros(acc_sc.shape, F32)

    qlat = qlat_ref[...]
    qpe = qpe_ref[...]
    pages = [ckv_refs[k][...].astype(BF16) for k in range(pps)]
    s = jnp.concatenate(
        [_nt_dot(qlat, pages[k]) + _nt_dot(qpe, kpe_refs[k][...].astype(BF16)) for k in range(pps)],
        axis=1) * ATTN_SCALE
    m_old = m_sc[...]
    m_new = jnp.maximum(m_old, jnp.max(s, axis=-1, keepdims=True))
    alpha = jnp.exp(m_old - m_new)
    p = jnp.exp(s - m_new)
    l_sc[...] = alpha * l_sc[...] + jnp.sum(p, axis=-1, keepdims=True)
    pv = jnp.dot(p[:, :PAGE_SIZE].astype(BF16), pages[0], preferred_element_type=F32)
    for k in range(1, pps):
        pv += jnp.dot(p[:, k * PAGE_SIZE:(k + 1) * PAGE_SIZE].astype(BF16), pages[k],
                      preferred_element_type=F32)
    acc_sc[...] = alpha * acc_sc[...] + pv
    m_sc[...] = m_new

    @pl.when(c == pl.num_programs(1) - 1)
    def _():
        qlf = qlat.astype(F32)
        qpf = qpe.astype(F32)
        cn = ckvn_ref[...].astype(BF16).astype(F32)
        kn = kpen_ref[...].astype(BF16).astype(F32)
        tok = lax.broadcasted_iota(jnp.int32, (rows, 1), 0) // MLA_HEADS
        neg = jnp.finfo(F32).min
        s_new = []
        for j in range(t_new):
            sj = (jnp.sum(qlf * cn[j:j + 1, :], axis=-1, keepdims=True)
                  + jnp.sum(qpf * kn[j:j + 1, :], axis=-1, keepdims=True))
            s_new.append(jnp.where(tok >= j, sj, neg) * ATTN_SCALE)
        m1 = m_sc[...]
        m2 = m1
        for sj in s_new:
            m2 = jnp.maximum(m2, sj)
        a2 = jnp.exp(m1 - m2)
        l2 = a2 * l_sc[...]
        acc2 = a2 * acc_sc[...]
        for j, sj in enumerate(s_new):
            pj = jnp.exp(sj - m2)
            l2 = l2 + pj
            acc2 = acc2 + pj.astype(BF16).astype(F32) * cn[j:j + 1, :]
        o_ref[...] = (acc2 / l2).astype(o_ref.dtype)


def _paged_attention(page_table, qlat, qpe, ckv_new, kpe_new, cache_ckv, cache_kpe, t_new, pps):
    dec_batch, n_pages = page_table.shape
    rows = t_new * MLA_HEADS
    n_chunks = n_pages // pps

    def page_spec(width, k):
        return pl.BlockSpec((None, PAGE_SIZE, width), lambda b, c, pt: (pt[b, c * pps + k], 0, 0))

    in_specs = [pl.BlockSpec((rows, KV_LORA), lambda b, c, pt: (b, 0)),
                pl.BlockSpec((rows, QK_ROPE), lambda b, c, pt: (b, 0)),
                pl.BlockSpec((None, t_new, KV_LORA), lambda b, c, pt: (b, 0, 0)),
                pl.BlockSpec((None, t_new, QK_ROPE), lambda b, c, pt: (b, 0, 0))]
    in_specs += [page_spec(KV_LORA, k) for k in range(pps)]
    in_specs += [page_spec(QK_ROPE, k) for k in range(pps)]
    grid_spec = pltpu.PrefetchScalarGridSpec(
        num_scalar_prefetch=1,
        grid=(dec_batch, n_chunks),
        in_specs=in_specs,
        out_specs=pl.BlockSpec((rows, KV_LORA), lambda b, c, pt: (b, 0)),
        scratch_shapes=[pltpu.VMEM((rows, 1), F32), pltpu.VMEM((rows, 1), F32),
                        pltpu.VMEM((rows, KV_LORA), F32)],
    )
    return pl.pallas_call(
        functools.partial(_paged_attn_kernel, pps=pps, t_new=t_new),
        grid_spec=grid_spec,
        out_shape=jax.ShapeDtypeStruct((dec_batch * rows, KV_LORA), BF16),
        compiler_params=_cparams("parallel", "arbitrary"),
        name="paged_attention",
    )(page_table, qlat, qpe, ckv_new, kpe_new, *([cache_ckv] * pps), *([cache_kpe] * pps))


def _uv_proj_kernel(olat_ref, wuv_ref, o_ref):
    for h in range(MLA_HEADS):
        o_ref[:, h * V_HEAD:(h + 1) * V_HEAD] = jnp.dot(
            olat_ref[:, h * KV_LORA:(h + 1) * KV_LORA], wuv_ref[h],
            preferred_element_type=F32).astype(o_ref.dtype)


def _uv_proj(olat, wuv, tm):
    n = olat.shape[0]
    return pl.pallas_call(
        _uv_proj_kernel,
        grid=(n // tm,),
        in_specs=[pl.BlockSpec((tm, MLA_HEADS * KV_LORA), lambda i: (i, 0)), _const_spec(wuv.shape)],
        out_specs=pl.BlockSpec((tm, MLA_HEADS * V_HEAD), lambda i: (i, 0)),
        out_shape=jax.ShapeDtypeStruct((n, MLA_HEADS * V_HEAD), BF16),
        compiler_params=_cparams("parallel"),
        name="uv_proj",
    )(olat, wuv)


def _split3(x):
    hi = x.astype(BF16)
    r = x - hi.astype(F32)
    mid = r.astype(BF16)
    lo = (r - mid.astype(F32)).astype(BF16)
    return hi, mid, lo


def _ssd_kernel(*refs, l_in, chunk, has_init):
    if has_init:
        (xbc_ref, dt_ref, conv0_ref, ssm0_ref, cw_ref, cb_ref, dtb_ref, alog_ref, dskip_ref,
         y_ref, ssm_ref, conv_ref, xpad, state, dtpad) = refs
    else:
        (xbc_ref, dt_ref, cw_ref, cb_ref, dtb_ref, alog_ref, dskip_ref,
         y_ref, ssm_ref, conv_ref, xpad, state, dtpad) = refs
    c = pl.program_id(1)
    last = pl.num_programs(1) - 1
    tail = SUBLANES

    @pl.when(c == 0)
    def _():
        if has_init:
            xpad[0:tail, :] = conv0_ref[...]
            state[...] = ssm0_ref[...]
        else:
            xpad[0:tail, :] = jnp.zeros((tail, CONV_DIM), F32)
            state[...] = jnp.zeros(state.shape, F32)
        if l_in < chunk:
            xpad[tail:, :] = jnp.zeros((chunk, CONV_DIM), F32)

    xpad[tail:tail + l_in, :] = xbc_ref[...]

    conv = cb_ref[...] + cw_ref[CONV_WIDTH - 1:CONV_WIDTH, :] * xpad[tail:tail + chunk, :]
    for k in range(CONV_WIDTH - 1):
        off = tail - (CONV_WIDTH - 1) + k
        conv = conv + cw_ref[k:k + 1, :] * xpad[off:off + chunk, :]
    xc = _silu(conv)

    win = tail + ((l_in - (CONV_WIDTH - 1)) // SUBLANES) * SUBLANES
    sub = (l_in - (CONV_WIDTH - 1)) % SUBLANES

    @pl.when(c == last)
    def _():
        conv_ref[...] = xpad[win:win + SUBLANES, :][sub:sub + CONV_WIDTH - 1, :]

    xpad[0:tail, :] = xpad[chunk:chunk + tail, :]

    rows = lax.broadcasted_iota(jnp.int32, (chunk, LANES), 0)
    if l_in < chunk:
        dtpad[...] = jnp.zeros((chunk, LANES), F32)
        dtpad[0:l_in, :] = dt_ref[...]
        dt_raw = dtpad[...]
    else:
        dt_raw = dt_ref[...]
    dt = jnp.where(rows < l_in, _softplus(dt_raw + dtb_ref[...]), 0.0)
    a = -jnp.exp(alog_ref[...])
    da = dt * a

    ti = lax.broadcasted_iota(jnp.int32, (chunk, chunk), 0)
    si = lax.broadcasted_iota(jnp.int32, (chunk, chunk), 1)
    tril = ti >= si
    tril_b = tril.astype(BF16)
    hi, mid, lo = _split3(da)
    cum3 = jnp.dot(tril_b, jnp.concatenate([hi, mid, lo], axis=1), preferred_element_type=F32)
    a_cum = (cum3[:, 0:LANES] + cum3[:, LANES:2 * LANES]) + cum3[:, 2 * LANES:3 * LANES]
    a_cum_t = a_cum.T
    exp_cum = jnp.exp(a_cum)
    a_last = a_cum[chunk - 1:chunk, :]
    dte = jnp.exp(a_last - a_cum)
    chunk_decay_t = jnp.exp(a_cum_t[:, chunk - 1:chunk])

    gw = HEADS_PER_GROUP * SSM_HEADDIM
    b_off = D_INNER
    c_off = D_INNER + SSM_GROUPS * SSM_STATE
    for g in range(SSM_GROUPS):
        bg = xc[:, b_off + g * SSM_STATE:b_off + (g + 1) * SSM_STATE].astype(BF16)
        cg = xc[:, c_off + g * SSM_STATE:c_off + (g + 1) * SSM_STATE].astype(BF16)
        cb = _nt_dot(cg, bg)
        st_g = state[g * gw:(g + 1) * gw, :]
        y_off = _nt_dot(cg, st_g.astype(BF16))
        xdte_parts = []
        for hl in range(HEADS_PER_GROUP):
            h = g * HEADS_PER_GROUP + hl
            xs = xc[:, h * SSM_HEADDIM:(h + 1) * SSM_HEADDIM]
            xdt = xs * dt[:, h:h + 1]
            seg = a_cum[:, h:h + 1] - a_cum_t[h:h + 1, :]
            decay = jnp.exp(jnp.where(tril, seg, -jnp.inf))
            y = jnp.dot((cb * decay).astype(BF16), xdt.astype(BF16), preferred_element_type=F32)
            y = y + y_off[:, hl * SSM_HEADDIM:(hl + 1) * SSM_HEADDIM] * exp_cum[:, h:h + 1]
            y = y + xs * dskip_ref[:, h * SSM_HEADDIM:(h + 1) * SSM_HEADDIM]
            y_ref[:, h * SSM_HEADDIM:(h + 1) * SSM_HEADDIM] = y[0:l_in, :]
            xdte_parts.append(xdt * dte[:, h:h + 1])
        xdte_t = jnp.concatenate(xdte_parts, axis=1).T.astype(BF16)
        cs = jnp.dot(xdte_t, bg, preferred_element_type=F32)
        for hl in range(HEADS_PER_GROUP):
            h = g * HEADS_PER_GROUP + hl
            r0 = g * gw + hl * SSM_HEADDIM
            state[r0:r0 + SSM_HEADDIM, :] = (st_g[hl * SSM_HEADDIM:(hl + 1) * SSM_HEADDIM, :]
                                             * chunk_decay_t[h:h + 1, :]
                                             + cs[hl * SSM_HEADDIM:(hl + 1) * SSM_HEADDIM, :])

    @pl.when(c == last)
    def _():
        ssm_ref[...] = state[...]


def _ssd(proj3, conv0, ssm0, cw, cb, dtb, alog, dskip, l_in, chunk):
    batch, seq, _ = proj3.shape
    nc = seq // l_in
    assert l_in == chunk or nc == 1
    has_init = conv0 is not None
    hp = SSM_HEADS * SSM_HEADDIM
    dt_blk = (PROJ_SMALL + SMALL_DT) // LANES
    in_specs = [pl.BlockSpec((None, l_in, CONV_DIM), lambda b, c: (b, c, PROJ_XBC // CONV_DIM)),
                pl.BlockSpec((None, l_in, LANES), lambda b, c: (b, c, dt_blk))]
    args = [proj3, proj3]
    if has_init:
        in_specs += [pl.BlockSpec((None, SUBLANES, CONV_DIM), lambda b, c: (b, 0, 0)),
                     pl.BlockSpec((None, hp, SSM_STATE), lambda b, c: (b, 0, 0))]
        args += [conv0, ssm0]
    consts = [cw, cb, dtb, alog, dskip]
    in_specs += [_const_spec(x.shape) for x in consts]
    args += consts
    return pl.pallas_call(
        functools.partial(_ssd_kernel, l_in=l_in, chunk=chunk, has_init=has_init),
        grid=(batch, nc),
        in_specs=in_specs,
        out_specs=[pl.BlockSpec((None, l_in, D_INNER), lambda b, c: (b, c, 0)),
                   pl.BlockSpec((None, hp, SSM_STATE), lambda b, c: (b, 0, 0)),
                   pl.BlockSpec((None, CONV_WIDTH - 1, CONV_DIM), lambda b, c: (b, 0, 0))],
        out_shape=[jax.ShapeDtypeStruct((batch, seq, D_INNER), F32),
                   jax.ShapeDtypeStruct((batch, hp, SSM_STATE), F32),
                   jax.ShapeDtypeStruct((batch, CONV_WIDTH - 1, CONV_DIM), F32)],
        scratch_shapes=[pltpu.VMEM((SUBLANES + chunk, CONV_DIM), F32), pltpu.VMEM((hp, SSM_STATE), F32),
                        pltpu.VMEM((chunk, LANES), F32)],
        compiler_params=_cparams("parallel", "arbitrary"),
        name="conv_ssd",
    )(*args)


def _layer_norm(x, g, b):
    mu = jnp.mean(x, axis=-1, keepdims=True)
    xc = x - mu
    var = jnp.mean(xc * xc, axis=-1, keepdims=True)
    return xc * lax.rsqrt(var + 1e-5) * g + b


def _merge_kernel(x_ref, attn_ref, y_ref, z_ref, g_ref, bg_ref, ng_ref, wa_ref, wb_ref, wo_ref,
                  l1g_ref, l1b_ref, h_ref):
    gsz = D_INNER // SSM_GROUPS
    yg = y_ref[...] * _silu(z_ref[...])
    parts = []
    for g in range(SSM_GROUPS):
        parts.append(_rms(yg[:, g * gsz:(g + 1) * gsz], ng_ref[:, g * gsz:(g + 1) * gsz], 1e-5).astype(BF16))
    ygn = jnp.concatenate(parts, axis=1)
    br_a = jnp.dot(attn_ref[...], wa_ref[...], preferred_element_type=F32)
    br_b = jnp.dot(ygn, wb_ref[...], preferred_element_type=F32)
    gates = _sigmoid(g_ref[...] + bg_ref[...])
    merged = gates[:, :D_MODEL] * br_a + gates[:, D_MODEL:] * br_b
    mix = jnp.dot(merged.astype(BF16), wo_ref[...], preferred_element_type=F32)
    h_ref[...] = _layer_norm(ALPHA * x_ref[...] + mix, l1g_ref[...], l1b_ref[...])


def _merge(x, attn, y, proj, bg, ng, wa, wb, wo, l1g, l1b, tm):
    n = x.shape[0]
    row = lambda w: pl.BlockSpec((tm, w), lambda i: (i, 0))
    consts = [bg, ng, wa, wb, wo, l1g, l1b]
    return pl.pallas_call(
        _merge_kernel,
        grid=(n // tm,),
        in_specs=[row(D_MODEL), row(MLA_HEADS * V_HEAD), row(D_INNER),
                  pl.BlockSpec((tm, D_INNER), lambda i: (i, PROJ_Z // D_INNER)),
                  pl.BlockSpec((tm, 2 * D_MODEL), lambda i: (i, PROJ_G // (2 * D_MODEL)))]
                 + [_const_spec(c.shape) for c in consts],
        out_specs=row(D_MODEL),
        out_shape=jax.ShapeDtypeStruct((n, D_MODEL), F32),
        compiler_params=_cparams("parallel"),
        name="merge_ln1",
    )(x, attn, y, proj, proj, *consts)


def _ffn_kernel(h_ref, wg_ref, wu_ref, wd_ref, l2g_ref, l2b_ref, o_ref):
    h = h_ref[...]
    hb = h.astype(BF16)
    g = jnp.dot(hb, wg_ref[...], preferred_element_type=F32)
    u = jnp.dot(hb, wu_ref[...], preferred_element_type=F32)
    f = jnp.dot((_silu(g) * u).astype(BF16), wd_ref[...], preferred_element_type=F32)
    o_ref[...] = _layer_norm(ALPHA * h + f, l2g_ref[...], l2b_ref[...])


def _ffn(h, wg, wu, wd, l2g, l2b, tm):
    n = h.shape[0]
    row = pl.BlockSpec((tm, D_MODEL), lambda i: (i, 0))
    consts = [wg, wu, wd, l2g, l2b]
    return pl.pallas_call(
        _ffn_kernel,
        grid=(n // tm,),
        in_specs=[row] + [_const_spec(c.shape) for c in consts],
        out_specs=row,
        out_shape=jax.ShapeDtypeStruct((n, D_MODEL), F32),
        compiler_params=_cparams("parallel"),
        name="ffn_ln2",
    )(h, *consts)


def _rot_half(w):
    half = QK_ROPE // 2
    return jnp.concatenate([-w[..., half:], w[..., :half]], axis=-1)


def _prep_weights(w_in, w_uq, w_ukv):
    wq, wkv, wkpe, wz, wxbc, wdt, wg = jnp.split(w_in, IN_OFFSETS, axis=1)
    zc = lambda n: jnp.zeros((D_MODEL, n), F32)
    pad_hi = LANES - PE_LANE0 - QK_ROPE
    small = jnp.concatenate([wq, wkv, zc(PE_LANE0), wkpe, zc(pad_hi), zc(PE_LANE0), _rot_half(wkpe), zc(pad_hi),
                             wdt, zc(LANES - SSM_HEADS)], axis=1)
    w_in_p = jnp.concatenate([wxbc, small, wz, wg], axis=1).astype(BF16)

    uq = w_uq.reshape(Q_LORA, MLA_HEADS, QK_NOPE + QK_ROPE)
    nope, pe = uq[..., :QK_NOPE], uq[..., QK_NOPE:]
    rot = _rot_half(pe)
    zq = lambda n: jnp.zeros((Q_LORA, MLA_HEADS, n), F32)
    hw = MLA_HEADS * LANES
    wq_a = jnp.concatenate([nope, pe, zq(pad_hi)], axis=-1).reshape(Q_LORA, hw)
    wq_b = jnp.concatenate([zq(PE_LANE0), rot, zq(pad_hi)], axis=-1).reshape(Q_LORA, hw)
    wq_prompt = jnp.concatenate([wq_a, wq_b], axis=1).astype(BF16)
    wq_n = jnp.concatenate([nope, zq(LANES - QK_NOPE)], axis=-1).reshape(Q_LORA, hw)
    wq_sample = jnp.concatenate([wq_n, pe.reshape(Q_LORA, -1), rot.reshape(Q_LORA, -1)], axis=1).astype(BF16)

    ukv = w_ukv.reshape(KV_LORA, MLA_HEADS, QK_NOPE + V_HEAD)
    k_nope, v = ukv[..., :QK_NOPE], ukv[..., QK_NOPE:]
    zk = jnp.zeros((KV_LORA, MLA_HEADS, LANES - QK_NOPE), F32)
    wkv_prompt = jnp.concatenate([jnp.concatenate([k_nope, zk], axis=-1).reshape(KV_LORA, hw),
                                  v.reshape(KV_LORA, -1)], axis=1).astype(BF16)
    wuk_t = jnp.concatenate([k_nope, zk], axis=-1).transpose(1, 2, 0).astype(BF16)
    wuv = v.transpose(1, 0, 2).astype(BF16)
    return w_in_p, wq_prompt, wq_sample, wkv_prompt, wuk_t, wuv


def _rope_tables(pos):
    half = QK_ROPE // 2
    inv = ROPE_THETA ** (-jnp.arange(0, QK_ROPE, 2, dtype=F32) / QK_ROPE)
    ang = pos.astype(F32)[:, None] * inv[None, :]
    c2 = jnp.concatenate([jnp.cos(ang), jnp.cos(ang)], axis=1)
    s2 = jnp.concatenate([jnp.sin(ang), jnp.sin(ang)], axis=1)
    t = pos.shape[0]
    cos = jnp.concatenate([jnp.ones((t, PE_LANE0), F32), c2, jnp.zeros((t, LANES - PE_LANE0 - QK_ROPE), F32)], axis=1)
    sin = jnp.concatenate([jnp.zeros((t, PE_LANE0), F32), s2, jnp.zeros((t, LANES - PE_LANE0 - QK_ROPE), F32)], axis=1)
    return cos, sin, jnp.tile(c2, (1, MLA_HEADS)), jnp.tile(s2, (1, MLA_HEADS))


def _tile_rows(n, pref):
    t = min(pref, n)
    while n % t:
        t //= 2
    return t


def kernel(x_prompt, x_sample, cache_ckv, cache_kpe, state_ssm, state_conv, page_table, w_in, b_gate, q_norm_g, kv_norm_g, w_uq, w_ukv, conv_w, conv_b, dt_bias, a_log, d_skip, ssm_norm_g, w_branch_a, w_branch_b, w_out, ln1_g, ln1_b, w_ffn_gate, w_ffn_up, w_ffn_down, ln2_g, ln2_b):
    assert w_in.shape[0] == DEPTH == 1
    batch, seq, _ = x_prompt.shape
    dec_batch, t_new, _ = x_sample.shape
    n_pages = page_table.shape[1]
    past_len = n_pages * PAGE_SIZE
    n_p = batch * seq
    n_s = dec_batch * t_new

    w_in_p, wq_prompt, wq_sample, wkv_prompt, wuk_t, wuv = _prep_weights(w_in[0], w_uq[0], w_ukv[0])
    row2 = lambda v: v.reshape(1, -1).astype(F32)
    gq, gkv = row2(q_norm_g[0]), row2(kv_norm_g[0])
    pad_heads = lambda v: jnp.concatenate([v.astype(F32), jnp.zeros((LANES - SSM_HEADS,), F32)]).reshape(1, LANES)
    dtb, alog = pad_heads(dt_bias[0]), pad_heads(a_log[0])
    dskip = jnp.repeat(d_skip[0].astype(F32), SSM_HEADDIM).reshape(1, D_INNER)
    cw, cb = conv_w[0].astype(F32), row2(conv_b[0])
    bg, ng = row2(b_gate[0]), row2(ssm_norm_g[0])
    wa, wb, wo = w_branch_a[0].astype(BF16), w_branch_b[0].astype(BF16), w_out[0].astype(BF16)
    wfg, wfu, wfd = w_ffn_gate[0].astype(BF16), w_ffn_up[0].astype(BF16), w_ffn_down[0].astype(BF16)
    l1g, l1b, l2g, l2b = row2(ln1_g[0]), row2(ln1_b[0]), row2(ln2_g[0]), row2(ln2_b[0])

    xp = x_prompt.reshape(n_p, D_MODEL)
    proj_p = _matmul(xp, w_in_p, _tile_rows(n_p, 1024), 1024, F32)
    cos_p, sin_p, _, _ = _rope_tables(jnp.arange(seq))
    q_p, k_p, v_p, ckv_p, kpe_p = _mla_prep_prompt(proj_p, cos_p, sin_p, gq, gkv, wq_prompt, wkv_prompt,
                                                   _tile_rows(seq, 256))
    attn_p = _prompt_attention(q_p, k_p, v_p, batch, seq, _tile_rows(seq, 256))
    chunk_p = min(SSD_CHUNK, seq)
    y_p, ssm_p, conv_p = _ssd(proj_p.reshape(batch, seq, PROJ_COLS), None, None, cw, cb, dtb, alog, dskip,
                              chunk_p, chunk_p)
    h_p = _merge(xp, attn_p, y_p.reshape(n_p, D_INNER), proj_p, bg, ng, wa, wb, wo, l1g, l1b, _tile_rows(n_p, 256))
    out_p = _ffn(h_p, wfg, wfu, wfd, l2g, l2b, _tile_rows(n_p, 256))

    xs = x_sample.reshape(n_s, D_MODEL)
    proj_s = _matmul(xs, w_in_p, _tile_rows(n_s, 512), 1024, F32)
    cos_s, sin_s, cosp_s, sinp_s = _rope_tables(past_len + jnp.arange(t_new))
    tm_s = t_new * _tile_rows(dec_batch, 64)
    rep = lambda t: jnp.tile(t, (tm_s // t_new, 1))
    qlat_s, qpe_s, ckv_s, kpe_s = _mla_prep_sample(proj_s, rep(cos_s), rep(sin_s), rep(cosp_s), rep(sinp_s),
                                                   gq, gkv, wq_sample, wuk_t, tm_s)
    pps = 16 if n_pages % 16 == 0 else 1
    olat_s = _paged_attention(page_table, qlat_s.reshape(n_s * MLA_HEADS, KV_LORA),
                              qpe_s.reshape(n_s * MLA_HEADS, QK_ROPE),
                              ckv_s.reshape(dec_batch, t_new, KV_LORA), kpe_s.reshape(dec_batch, t_new, QK_ROPE),
                              cache_ckv.reshape(cache_ckv.shape[1:]), cache_kpe.reshape(cache_kpe.shape[1:]), t_new, pps)
    attn_s = _uv_proj(olat_s.reshape(n_s, MLA_HEADS * KV_LORA), wuv, _tile_rows(n_s, 256))
    conv0 = jnp.concatenate([jnp.zeros((dec_batch, SUBLANES - (CONV_WIDTH - 1), CONV_DIM), F32),
                             state_conv[0].astype(F32)], axis=1)
    ssm0 = state_ssm.astype(F32).reshape(dec_batch, SSM_HEADS * SSM_HEADDIM, SSM_STATE)
    y_s, ssm_s, conv_s = _ssd(proj_s.reshape(dec_batch, t_new, PROJ_COLS), conv0, ssm0, cw, cb, dtb, alog, dskip,
                              t_new, LANES)
    h_s = _merge(xs, attn_s, y_s.reshape(n_s, D_INNER), proj_s, bg, ng, wa, wb, wo, l1g, l1b, _tile_rows(n_s, 256))
    out_s = _ffn(h_s, wfg, wfu, wfd, l2g, l2b, _tile_rows(n_s, 256))

    st_shape = (SSM_HEADS, SSM_HEADDIM, SSM_STATE)
    return (out_p.reshape(batch, seq, D_MODEL), out_s.reshape(dec_batch, t_new, D_MODEL),
            ckv_p.reshape(1, batch, seq, KV_LORA), kpe_p.reshape(1, batch, seq, QK_ROPE),
            ssm_p.reshape(1, batch, *st_shape), conv_p.reshape(1, batch, CONV_WIDTH - 1, CONV_DIM),
            ckv_s.reshape(1, dec_batch, t_new, KV_LORA), kpe_s.reshape(1, dec_batch, t_new, QK_ROPE),
            ssm_s.reshape(1, dec_batch, *st_shape), conv_s.reshape(1, dec_batch, CONV_WIDTH - 1, CONV_DIM))
```

```python
import functools
import math

import jax
import jax.numpy as jnp
from jax import lax
from jax.experimental import pallas as pl
from jax.experimental.pallas import tpu as pltpu

F32 = jnp.float32
BF16 = jnp.bfloat16

D_MODEL = 1024
MLA_HEADS = 16
QK_NOPE = 64
QK_ROPE = 32
V_HEAD = 64
Q_LORA = 384
KV_LORA = 256
ROPE_THETA = 10000.0
ATTN_SCALE = (QK_NOPE + QK_ROPE) ** -0.5
Q_PRESCALE = ATTN_SCALE * math.log2(math.e)
D_INNER = 2 * D_MODEL
SSM_HEADDIM = 64
SSM_HEADS = D_INNER // SSM_HEADDIM
SSM_GROUPS = 4
HEADS_PER_GROUP = SSM_HEADS // SSM_GROUPS
SSM_STATE = 128
CONV_WIDTH = 4
CONV_DIM = D_INNER + 2 * SSM_GROUPS * SSM_STATE
SSD_CHUNK = 256
D_FF = ((8 * D_MODEL // 3 + 255) // 256) * 256
PAGE_SIZE = 128
DEPTH = 1
ALPHA = (2.0 * DEPTH) ** 0.25
IN_SPLITS = (Q_LORA, KV_LORA, QK_ROPE, D_INNER, CONV_DIM, SSM_HEADS, 2 * D_MODEL)
IN_OFFSETS = tuple(sum(IN_SPLITS[:i + 1]) for i in range(len(IN_SPLITS) - 1))

LANES = 128
SUBLANES = 8
VMEM_LIMIT_BYTES = 56 * 1024 * 1024

PROJ_XBC = 0
PROJ_SMALL = CONV_DIM
PROJ_Z = 4096
PROJ_G = 6144
PROJ_COLS = 8192
SMALL_CQ = 0
SMALL_CKV = Q_LORA
SMALL_KPE = Q_LORA + KV_LORA
SMALL_KROT = SMALL_KPE + LANES
SMALL_DT = SMALL_KROT + LANES
SMALL_COLS = 1024
PE_LANE0 = QK_NOPE


def _cparams(*sem):
    return pltpu.CompilerParams(dimension_semantics=sem, vmem_limit_bytes=VMEM_LIMIT_BYTES)


def _nt_dot(a, b):
    return lax.dot_general(a, b, (((1,), (1,)), ((), ())), preferred_element_type=F32)


def _silu(x):
    return x * (1.0 / (1.0 + jnp.exp(-x)))


def _sigmoid(x):
    return 1.0 / (1.0 + jnp.exp(-x))


def _softplus(x):
    return jnp.maximum(x, 0.0) + jnp.log(1.0 + jnp.exp(-jnp.abs(x)))


def _matmul_kernel(x_ref, w_ref, o_ref):
    o_ref[...] = jnp.dot(x_ref[...].astype(BF16), w_ref[...],
                         preferred_element_type=F32).astype(o_ref.dtype)


def _matmul(x, w, tm, tn, out_dtype):
    m, k = x.shape
    n = w.shape[1]
    return pl.pallas_call(
        _matmul_kernel,
        grid=(m // tm, n // tn),
        in_specs=[pl.BlockSpec((tm, k), lambda i, j: (i, 0)),
                  pl.BlockSpec((k, tn), lambda i, j: (0, j))],
        out_specs=pl.BlockSpec((tm, tn), lambda i, j: (i, j)),
        out_shape=jax.ShapeDtypeStruct((m, n), out_dtype),
        compiler_params=_cparams("parallel", "arbitrary"),
        name="in_proj",
    )(x, w)


def _rms(x, g, eps):
    return x * lax.rsqrt(jnp.mean(x * x, axis=-1, keepdims=True) + eps) * g


def _mla_common(blk, cos, sin, gq_ref, gkv_ref, ckv_ref, kpe_ref):
    cqn = _rms(blk[:, SMALL_CQ:SMALL_CQ + Q_LORA], gq_ref[...], 1e-6).astype(BF16)
    ckvn = _rms(blk[:, SMALL_CKV:SMALL_CKV + KV_LORA], gkv_ref[...], 1e-6)
    ckv_ref[...] = ckvn
    kr = blk[:, SMALL_KPE:SMALL_KPE + LANES] * cos + blk[:, SMALL_KROT:SMALL_KROT + LANES] * sin
    kpe_ref[...] = kr[:, PE_LANE0:PE_LANE0 + QK_ROPE]
    return cqn, ckvn, kr


def _mla_prep_prompt_kernel(blk_ref, cos_ref, sin_ref, gq_ref, gkv_ref, wq_ref, wkv_ref,
                            q_ref, k_ref, v_ref, ckv_ref, kpe_ref):
    cos = cos_ref[...]
    sin = sin_ref[...]
    cqn, ckvn, kr = _mla_common(blk_ref[...], cos, sin, gq_ref, gkv_ref, ckv_ref, kpe_ref)
    hw = MLA_HEADS * LANES
    qab = jnp.dot(cqn, wq_ref[...], preferred_element_type=F32)
    kv = jnp.dot(ckvn.astype(BF16), wkv_ref[...], preferred_element_type=F32)
    for h in range(MLA_HEADS):
        sl = slice(h * LANES, (h + 1) * LANES)
        q_ref[:, sl] = ((qab[:, sl] * cos + qab[:, hw + h * LANES:hw + (h + 1) * LANES] * sin)
                        * Q_PRESCALE).astype(BF16)
        k_ref[:, sl] = (kv[:, sl] + kr).astype(BF16)
    v_ref[...] = kv[:, hw:].astype(BF16)


def _mla_prep_sample_kernel(blk_ref, cos_ref, sin_ref, cosp_ref, sinp_ref, gq_ref, gkv_ref,
                            wq_ref, wuk_ref, qlat_ref, qpe_ref, ckv_ref, kpe_ref):
    cqn, _, _ = _mla_common(blk_ref[...], cos_ref[...], sin_ref[...], gq_ref, gkv_ref, ckv_ref, kpe_ref)
    hw = MLA_HEADS * LANES
    pw = MLA_HEADS * QK_ROPE
    qall = jnp.dot(cqn, wq_ref[...], preferred_element_type=F32)
    qpe_ref[...] = ((qall[:, hw:hw + pw] * cosp_ref[...] + qall[:, hw + pw:] * sinp_ref[...])
                    * Q_PRESCALE).astype(BF16)
    for h in range(MLA_HEADS):
        qn = qall[:, h * LANES:(h + 1) * LANES].astype(BF16)
        qlat_ref[:, h * KV_LORA:(h + 1) * KV_LORA] = (jnp.dot(
            qn, wuk_ref[h], preferred_element_type=F32) * Q_PRESCALE).astype(BF16)


def _const_spec(shape):
    nd = len(shape)
    return pl.BlockSpec(shape, lambda *_: (0,) * nd)


def _mla_prep_prompt(proj, cos, sin, gq, gkv, wq, wkv, tm):
    n = proj.shape[0]
    hw = MLA_HEADS * LANES
    small_blk = PROJ_SMALL // SMALL_COLS
    row = lambda w: pl.BlockSpec((tm, w), lambda i: (i, 0))
    period = cos.shape[0] // tm
    tab = lambda w: pl.BlockSpec((tm, w), lambda i: (i % period, 0))
    return pl.pallas_call(
        _mla_prep_prompt_kernel,
        grid=(n // tm,),
        in_specs=[pl.BlockSpec((tm, SMALL_COLS), lambda i: (i, small_blk)),
                  tab(LANES), tab(LANES),
                  _const_spec(gq.shape), _const_spec(gkv.shape),
                  _const_spec(wq.shape), _const_spec(wkv.shape)],
        out_specs=[row(hw), row(hw), row(MLA_HEADS * V_HEAD), row(KV_LORA), row(QK_ROPE)],
        out_shape=[jax.ShapeDtypeStruct((n, hw), BF16), jax.ShapeDtypeStruct((n, hw), BF16),
                   jax.ShapeDtypeStruct((n, MLA_HEADS * V_HEAD), BF16),
                   jax.ShapeDtypeStruct((n, KV_LORA), F32), jax.ShapeDtypeStruct((n, QK_ROPE), F32)],
        compiler_params=_cparams("parallel"),
        name="mla_prep_prompt",
    )(proj, cos, sin, gq, gkv, wq, wkv)


def _mla_prep_sample(proj, cos, sin, cosp, sinp, gq, gkv, wq, wuk, tm):
    n = proj.shape[0]
    pw = MLA_HEADS * QK_ROPE
    small_blk = PROJ_SMALL // SMALL_COLS
    row = lambda w: pl.BlockSpec((tm, w), lambda i: (i, 0))
    tab = lambda w: pl.BlockSpec((tm, w), lambda i: (0, 0))
    return pl.pallas_call(
        _mla_prep_sample_kernel,
        grid=(n // tm,),
        in_specs=[pl.BlockSpec((tm, SMALL_COLS), lambda i: (i, small_blk)),
                  tab(LANES), tab(LANES), tab(pw), tab(pw),
                  _const_spec(gq.shape), _const_spec(gkv.shape),
                  _const_spec(wq.shape), _const_spec(wuk.shape)],
        out_specs=[row(MLA_HEADS * KV_LORA), row(pw), row(KV_LORA), row(QK_ROPE)],
        out_shape=[jax.ShapeDtypeStruct((n, MLA_HEADS * KV_LORA), BF16),
                   jax.ShapeDtypeStruct((n, pw), BF16),
                   jax.ShapeDtypeStruct((n, KV_LORA), F32), jax.ShapeDtypeStruct((n, QK_ROPE), F32)],
        compiler_params=_cparams("parallel"),
        name="mla_prep_sample",
    )(proj, cos, sin, cosp, sinp, gq, gkv, wq, wuk)


def _flash_kernel(q_ref, k_ref, v_ref, o_ref, *, tq):
    qi = pl.program_id(2)
    q2 = q_ref[...]
    qs = (q2[:, :LANES], q2[:, LANES:])
    neg = jnp.finfo(F32).min

    def tile(j, carry, masked):
        start = pl.multiple_of(j * tq, tq)
        v = v_ref[pl.ds(start, tq), :]
        new = []
        for hh in range(2):
            m, l, acc = carry[hh]
            k = k_ref[pl.ds(start, tq), hh * LANES:(hh + 1) * LANES]
            s = _nt_dot(qs[hh], k)
            if masked:
                row = lax.broadcasted_iota(jnp.int32, (tq, tq), 0)
                col = lax.broadcasted_iota(jnp.int32, (tq, tq), 1)
                s = jnp.where(col <= row, s, neg)
            m_new = jnp.maximum(m, jnp.max(s, axis=-1, keepdims=True))
            alpha = jnp.exp2(m - m_new)
            p = jnp.exp2(s - m_new)
            l = alpha * l + jnp.sum(p, axis=-1, keepdims=True)
            acc = alpha * acc + jnp.dot(p.astype(BF16), v, preferred_element_type=F32)
            new.append((m_new, l, acc))
        return tuple(new)

    one = (jnp.full((tq, 1), -jnp.inf, F32), jnp.zeros((tq, 1), F32), jnp.zeros((tq, LANES), F32))
    carry = lax.fori_loop(0, qi, functools.partial(tile, masked=False), (one, one))
    (_, l0, acc0), (_, l1, acc1) = tile(qi, carry, True)
    lane = lax.broadcasted_iota(jnp.int32, (tq, LANES), 1)
    o_ref[...] = jnp.where(lane < V_HEAD, acc0 / l0, acc1 / l1).astype(o_ref.dtype)


def _prompt_attention(q, k, v, batch, seq, tq):
    n = batch * seq
    nq = seq // tq
    pairs = MLA_HEADS // 2
    return pl.pallas_call(
        functools.partial(_flash_kernel, tq=tq),
        grid=(batch, pairs, nq),
        in_specs=[pl.BlockSpec((tq, 2 * LANES), lambda b, p, i: (b * nq + i, p)),
                  pl.BlockSpec((seq, 2 * LANES), lambda b, p, i: (b, p)),
                  pl.BlockSpec((seq, 2 * V_HEAD), lambda b, p, i: (b, p))],
        out_specs=pl.BlockSpec((tq, 2 * V_HEAD), lambda b, p, i: (b * nq + i, p)),
        out_shape=jax.ShapeDtypeStruct((n, MLA_HEADS * V_HEAD), BF16),
        compiler_params=_cparams("parallel", "parallel", "arbitrary"),
        name="prompt_attention",
    )(q, k, v)


def _paged_attn_kernel(pt_ref, qlat_ref, qpe_ref, ckvn_ref, kpen_ref, ckv_hbm, kpe_hbm, o_ref,
                       ckv_buf, kpe_buf, sem, m_sc, l_sc, acc_sc, *, pps, t_new, n_chunks):
    step = pl.program_id(0)
    n_steps = pl.num_programs(0)
    c = step % n_chunks
    slot = step % 2
    rows = t_new * MLA_HEADS

    def page_copies(s, sl):
        out = []
        for k in range(pps):
            page = pt_ref[s * pps + k]
            tok = pl.ds(k * PAGE_SIZE, PAGE_SIZE)
            out.append(pltpu.make_async_copy(ckv_hbm.at[page], ckv_buf.at[sl, tok, :], sem.at[sl]))
            out.append(pltpu.make_async_copy(kpe_hbm.at[page], kpe_buf.at[sl, :, tok], sem.at[sl]))
        return out

    @pl.when(step == 0)
    def _():
        for cp in page_copies(step, slot):
            cp.start()

    @pl.when(step + 1 < n_steps)
    def _():
        for cp in page_copies(step + 1, 1 - slot):
            cp.start()

    @pl.when(c == 0)
    def _():
        m_sc[...] = jnp.full(m_sc.shape, -jnp.inf, F32)
        l_sc[...] = jnp.zeros(l_sc.shape, F32)
        acc_sc[...] = jnp.zeros(acc_sc.shape, F32)

    for cp in page_copies(step, slot):
        cp.wait()

    qlat = qlat_ref[...]
    qpe = qpe_ref[...]
    ck = ckv_buf[slot].astype(BF16)
    kp = kpe_buf[slot].astype(BF16)
    s = _nt_dot(qlat, ck) + jnp.dot(qpe, kp, preferred_element_type=F32)
    m_old = m_sc[...]
    m_new = jnp.maximum(m_old, jnp.max(s, axis=-1, keepdims=True))
    alpha = jnp.exp2(m_old - m_new)
    p = jnp.exp2(s - m_new)
    l_sc[...] = alpha * l_sc[...] + jnp.sum(p, axis=-1, keepdims=True)
    acc_sc[...] = alpha * acc_sc[...] + jnp.dot(p.astype(BF16), ck, preferred_element_type=F32)
    m_sc[...] = m_new

    @pl.when(c == n_chunks - 1)
    def _():
        qlf = qlat.astype(F32)
        qpf = qpe.astype(F32)
        cn = ckvn_ref[...].astype(BF16).astype(F32)
        kn = kpen_ref[...].astype(BF16).astype(F32)
        tok = lax.broadcasted_iota(jnp.int32, (rows, 1), 0) // MLA_HEADS
        neg = jnp.finfo(F32).min
        s_new = []
        for j in range(t_new):
            sj = (jnp.sum(qlf * cn[j:j + 1, :], axis=-1, keepdims=True)
                  + jnp.sum(qpf * kn[j:j + 1, :], axis=-1, keepdims=True))
            s_new.append(jnp.where(tok >= j, sj, neg))
        m1 = m_sc[...]
        m2 = m1
        for sj in s_new:
            m2 = jnp.maximum(m2, sj)
        a2 = jnp.exp2(m1 - m2)
        l2 = a2 * l_sc[...]
        acc2 = a2 * acc_sc[...]
        for j, sj in enumerate(s_new):
            pj = jnp.exp2(sj - m2)
            l2 = l2 + pj
            acc2 = acc2 + pj.astype(BF16).astype(F32) * cn[j:j + 1, :]
        o_ref[...] = (acc2 / l2).astype(o_ref.dtype)


def _paged_attention(page_table, qlat, qpe, ckv_new, kpe_new, cache_ckv, cache_kpe, t_new, pps):
    dec_batch, n_pages = page_table.shape
    rows = t_new * MLA_HEADS
    n_chunks = n_pages // pps
    by_seq = lambda s, pt: (s // n_chunks, 0)
    by_seq3 = lambda s, pt: (s // n_chunks, 0, 0)
    grid_spec = pltpu.PrefetchScalarGridSpec(
        num_scalar_prefetch=1,
        grid=(dec_batch * n_chunks,),
        in_specs=[pl.BlockSpec((rows, KV_LORA), by_seq),
                  pl.BlockSpec((rows, QK_ROPE), by_seq),
                  pl.BlockSpec((None, t_new, KV_LORA), by_seq3),
                  pl.BlockSpec((None, t_new, QK_ROPE), by_seq3),
                  pl.BlockSpec(memory_space=pl.ANY),
                  pl.BlockSpec(memory_space=pl.ANY)],
        out_specs=pl.BlockSpec((rows, KV_LORA), by_seq),
        scratch_shapes=[pltpu.VMEM((2, pps * PAGE_SIZE, KV_LORA), F32),
                        pltpu.VMEM((2, QK_ROPE, pps * PAGE_SIZE), F32),
                        pltpu.SemaphoreType.DMA((2,)),
                        pltpu.VMEM((rows, 1), F32), pltpu.VMEM((rows, 1), F32),
                        pltpu.VMEM((rows, KV_LORA), F32)],
    )
    return pl.pallas_call(
        functools.partial(_paged_attn_kernel, pps=pps, t_new=t_new, n_chunks=n_chunks),
        grid_spec=grid_spec,
        out_shape=jax.ShapeDtypeStruct((dec_batch * rows, KV_LORA), BF16),
        compiler_params=_cparams("arbitrary"),
        name="paged_attention",
    )(page_table.reshape(-1), qlat, qpe, ckv_new, kpe_new, cache_ckv, cache_kpe)


def _uv_proj_kernel(olat_ref, wuv_ref, o_ref):
    for h in range(MLA_HEADS):
        o_ref[:, h * V_HEAD:(h + 1) * V_HEAD] = jnp.dot(
            olat_ref[:, h * KV_LORA:(h + 1) * KV_LORA], wuv_ref[h],
            preferred_element_type=F32).astype(o_ref.dtype)


def _uv_proj(olat, wuv, tm):
    n = olat.shape[0]
    return pl.pallas_call(
        _uv_proj_kernel,
        grid=(n // tm,),
        in_specs=[pl.BlockSpec((tm, MLA_HEADS * KV_LORA), lambda i: (i, 0)), _const_spec(wuv.shape)],
        out_specs=pl.BlockSpec((tm, MLA_HEADS * V_HEAD), lambda i: (i, 0)),
        out_shape=jax.ShapeDtypeStruct((n, MLA_HEADS * V_HEAD), BF16),
        compiler_params=_cparams("parallel"),
        name="uv_proj",
    )(olat, wuv)


def _split3(x):
    hi = x.astype(BF16)
    r = x - hi.astype(F32)
    mid = r.astype(BF16)
    lo = (r - mid.astype(F32)).astype(BF16)
    return hi, mid, lo


def _ssd_kernel(*refs, l_in, chunk, has_init):
    if has_init:
        (xbc_ref, dt_ref, conv0_ref, ssm0_ref, cw_ref, cb_ref, dtb_ref, alog_ref, dskip_ref,
         y_ref, ssm_ref, conv_ref, xpad, state, dtpad) = refs
    else:
        (xbc_ref, dt_ref, cw_ref, cb_ref, dtb_ref, alog_ref, dskip_ref,
         y_ref, ssm_ref, conv_ref, xpad, state, dtpad) = refs
    c = pl.program_id(1)
    last = pl.num_programs(1) - 1
    tail = SUBLANES

    @pl.when(c == 0)
    def _():
        if has_init:
            xpad[0:tail, :] = conv0_ref[...]
            state[...] = ssm0_ref[...]
        else:
            xpad[0:tail, :] = jnp.zeros((tail, CONV_DIM), F32)
            state[...] = jnp.zeros(state.shape, F32)
        if l_in < chunk:
            xpad[tail:, :] = jnp.zeros((chunk, CONV_DIM), F32)

    xpad[tail:tail + l_in, :] = xbc_ref[...]

    conv = cb_ref[...] + cw_ref[CONV_WIDTH - 1:CONV_WIDTH, :] * xpad[tail:tail + chunk, :]
    for k in range(CONV_WIDTH - 1):
        off = tail - (CONV_WIDTH - 1) + k
        conv = conv + cw_ref[k:k + 1, :] * xpad[off:off + chunk, :]
    xc = _silu(conv)

    win = tail + ((l_in - (CONV_WIDTH - 1)) // SUBLANES) * SUBLANES
    sub = (l_in - (CONV_WIDTH - 1)) % SUBLANES

    @pl.when(c == last)
    def _():
        conv_ref[...] = xpad[win:win + SUBLANES, :][sub:sub + CONV_WIDTH - 1, :]

    xpad[0:tail, :] = xpad[chunk:chunk + tail, :]

    rows = lax.broadcasted_iota(jnp.int32, (chunk, LANES), 0)
    if l_in < chunk:
        dtpad[...] = jnp.zeros((chunk, LANES), F32)
        dtpad[0:l_in, :] = dt_ref[...]
        dt_raw = dtpad[...]
    else:
        dt_raw = dt_ref[...]
    dt = jnp.where(rows < l_in, _softplus(dt_raw + dtb_ref[...]), 0.0)
    a = -jnp.exp(alog_ref[...])
    da = dt * a

    ti = lax.broadcasted_iota(jnp.int32, (chunk, chunk), 0)
    si = lax.broadcasted_iota(jnp.int32, (chunk, chunk), 1)
    tril = ti >= si
    tril_b = tril.astype(BF16)
    hi, mid, lo = _split3(da)
    cum3 = jnp.dot(tril_b, jnp.concatenate([hi, mid, lo], axis=1), preferred_element_type=F32)
    a_cum = (cum3[:, 0:LANES] + cum3[:, LANES:2 * LANES]) + cum3[:, 2 * LANES:3 * LANES]
    a_cum_t = a_cum.T
    exp_cum = jnp.exp(a_cum)
    a_last = a_cum[chunk - 1:chunk, :]
    dte = jnp.exp(a_last - a_cum)
    chunk_decay_t = jnp.exp(a_cum_t[:, chunk - 1:chunk])

    gw = HEADS_PER_GROUP * SSM_HEADDIM
    b_off = D_INNER
    c_off = D_INNER + SSM_GROUPS * SSM_STATE
    for g in range(SSM_GROUPS):
        bg = xc[:, b_off + g * SSM_STATE:b_off + (g + 1) * SSM_STATE].astype(BF16)
        cg = xc[:, c_off + g * SSM_STATE:c_off + (g + 1) * SSM_STATE].astype(BF16)
        cb = _nt_dot(cg, bg)
        st_g = state[g * gw:(g + 1) * gw, :]
        y_off = _nt_dot(cg, st_g.astype(BF16))
        xdte_parts = []
        for hl in range(HEADS_PER_GROUP):
            h = g * HEADS_PER_GROUP + hl
            xs = xc[:, h * SSM_HEADDIM:(h + 1) * SSM_HEADDIM]
            xdt = xs * dt[:, h:h + 1]
            seg = a_cum[:, h:h + 1] - a_cum_t[h:h + 1, :]
            decay = jnp.exp(jnp.where(tril, seg, -jnp.inf))
            y = jnp.dot((cb * decay).astype(BF16), xdt.astype(BF16), preferred_element_type=F32)
            y = y + y_off[:, hl * SSM_HEADDIM:(hl + 1) * SSM_HEADDIM] * exp_cum[:, h:h + 1]
            y = y + xs * dskip_ref[:, h * SSM_HEADDIM:(h + 1) * SSM_HEADDIM]
            y_ref[:, h * SSM_HEADDIM:(h + 1) * SSM_HEADDIM] = y[0:l_in, :]
            xdte_parts.append(xdt * dte[:, h:h + 1])
        xdte_t = jnp.concatenate(xdte_parts, axis=1).T.astype(BF16)
        cs = jnp.dot(xdte_t, bg, preferred_element_type=F32)
        for hl in range(HEADS_PER_GROUP):
            h = g * HEADS_PER_GROUP + hl
            r0 = g * gw + hl * SSM_HEADDIM
            state[r0:r0 + SSM_HEADDIM, :] = (st_g[hl * SSM_HEADDIM:(hl + 1) * SSM_HEADDIM, :]
                                             * chunk_decay_t[h:h + 1, :]
                                             + cs[hl * SSM_HEADDIM:(hl + 1) * SSM_HEADDIM, :])

    @pl.when(c == last)
    def _():
        ssm_ref[...] = state[...]


def _ssd(proj3, conv0, ssm0, cw, cb, dtb, alog, dskip, l_in, chunk):
    batch, seq, _ = proj3.shape
    nc = seq // l_in
    assert l_in == chunk or nc == 1
    has_init = conv0 is not None
    hp = SSM_HEADS * SSM_HEADDIM
    dt_blk = (PROJ_SMALL + SMALL_DT) // LANES
    in_specs = [pl.BlockSpec((None, l_in, CONV_DIM), lambda b, c: (b, c, PROJ_XBC // CONV_DIM)),
                pl.BlockSpec((None, l_in, LANES), lambda b, c: (b, c, dt_blk))]
    args = [proj3, proj3]
    if has_init:
        in_specs += [pl.BlockSpec((None, SUBLANES, CONV_DIM), lambda b, c: (b, 0, 0)),
                     pl.BlockSpec((None, hp, SSM_STATE), lambda b, c: (b, 0, 0))]
        args += [conv0, ssm0]
    consts = [cw, cb, dtb, alog, dskip]
    in_specs += [_const_spec(x.shape) for x in consts]
    args += consts
    return pl.pallas_call(
        functools.partial(_ssd_kernel, l_in=l_in, chunk=chunk, has_init=has_init),
        grid=(batch, nc),
        in_specs=in_specs,
        out_specs=[pl.BlockSpec((None, l_in, D_INNER), lambda b, c: (b, c, 0)),
                   pl.BlockSpec((None, hp, SSM_STATE), lambda b, c: (b, 0, 0)),
                   pl.BlockSpec((None, CONV_WIDTH - 1, CONV_DIM), lambda b, c: (b, 0, 0))],
        out_shape=[jax.ShapeDtypeStruct((batch, seq, D_INNER), F32),
                   jax.ShapeDtypeStruct((batch, hp, SSM_STATE), F32),
                   jax.ShapeDtypeStruct((batch, CONV_WIDTH - 1, CONV_DIM), F32)],
        scratch_shapes=[pltpu.VMEM((SUBLANES + chunk, CONV_DIM), F32), pltpu.VMEM((hp, SSM_STATE), F32),
                        pltpu.VMEM((chunk, LANES), F32)],
        compiler_params=_cparams("parallel", "arbitrary"),
        name="conv_ssd",
    )(*args)


def _layer_norm(x, g, b):
    mu = jnp.mean(x, axis=-1, keepdims=True)
    xc = x - mu
    var = jnp.mean(xc * xc, axis=-1, keepdims=True)
    return xc * lax.rsqrt(var + 1e-5) * g + b


def _merge_kernel(x_ref, attn_ref, y_ref, z_ref, g_ref, bg_ref, ng_ref, wa_ref, wb_ref, wo_ref,
                  l1g_ref, l1b_ref, h_ref):
    gsz = D_INNER // SSM_GROUPS
    yg = y_ref[...] * _silu(z_ref[...])
    parts = []
    for g in range(SSM_GROUPS):
        parts.append(_rms(yg[:, g * gsz:(g + 1) * gsz], ng_ref[:, g * gsz:(g + 1) * gsz], 1e-5).astype(BF16))
    ygn = jnp.concatenate(parts, axis=1)
    br_a = jnp.dot(attn_ref[...], wa_ref[...], preferred_element_type=F32)
    br_b = jnp.dot(ygn, wb_ref[...], preferred_element_type=F32)
    gates = _sigmoid(g_ref[...] + bg_ref[...])
    merged = gates[:, :D_MODEL] * br_a + gates[:, D_MODEL:] * br_b
    mix = jnp.dot(merged.astype(BF16), wo_ref[...], preferred_element_type=F32)
    h_ref[...] = _layer_norm(ALPHA * x_ref[...] + mix, l1g_ref[...], l1b_ref[...])


def _merge(x, attn, y, proj, bg, ng, wa, wb, wo, l1g, l1b, tm):
    n = x.shape[0]
    row = lambda w: pl.BlockSpec((tm, w), lambda i: (i, 0))
    consts = [bg, ng, wa, wb, wo, l1g, l1b]
    return pl.pallas_call(
        _merge_kernel,
        grid=(n // tm,),
        in_specs=[row(D_MODEL), row(MLA_HEADS * V_HEAD), row(D_INNER),
                  pl.BlockSpec((tm, D_INNER), lambda i: (i, PROJ_Z // D_INNER)),
                  pl.BlockSpec((tm, 2 * D_MODEL), lambda i: (i, PROJ_G // (2 * D_MODEL)))]
                 + [_const_spec(c.shape) for c in consts],
        out_specs=row(D_MODEL),
        out_shape=jax.ShapeDtypeStruct((n, D_MODEL), F32),
        compiler_params=_cparams("parallel"),
        name="merge_ln1",
    )(x, attn, y, proj, proj, *consts)


def _ffn_kernel(h_ref, wg_ref, wu_ref, wd_ref, l2g_ref, l2b_ref, o_ref):
    h = h_ref[...]
    hb = h.astype(BF16)
    g = jnp.dot(hb, wg_ref[...], preferred_element_type=F32)
    u = jnp.dot(hb, wu_ref[...], preferred_element_type=F32)
    f = jnp.dot((_silu(g) * u).astype(BF16), wd_ref[...], preferred_element_type=F32)
    o_ref[...] = _layer_norm(ALPHA * h + f, l2g_ref[...], l2b_ref[...])


def _ffn(h, wg, wu, wd, l2g, l2b, tm):
    n = h.shape[0]
    row = pl.BlockSpec((tm, D_MODEL), lambda i: (i, 0))
    consts = [wg, wu, wd, l2g, l2b]
    return pl.pallas_call(
        _ffn_kernel,
        grid=(n // tm,),
        in_specs=[row] + [_const_spec(c.shape) for c in consts],
        out_specs=row,
        out_shape=jax.ShapeDtypeStruct((n, D_MODEL), F32),
        compiler_params=_cparams("parallel"),
        name="ffn_ln2",
    )(h, *consts)


def _rot_half(w):
    half = QK_ROPE // 2
    return jnp.concatenate([-w[..., half:], w[..., :half]], axis=-1)


def _prep_weights(w_in, w_uq, w_ukv):
    wq, wkv, wkpe, wz, wxbc, wdt, wg = jnp.split(w_in, IN_OFFSETS, axis=1)
    zc = lambda n: jnp.zeros((D_MODEL, n), F32)
    pad_hi = LANES - PE_LANE0 - QK_ROPE
    small = jnp.concatenate([wq, wkv, zc(PE_LANE0), wkpe, zc(pad_hi), zc(PE_LANE0), _rot_half(wkpe), zc(pad_hi),
                             wdt, zc(LANES - SSM_HEADS)], axis=1)
    w_in_p = jnp.concatenate([wxbc, small, wz, wg], axis=1).astype(BF16)

    uq = w_uq.reshape(Q_LORA, MLA_HEADS, QK_NOPE + QK_ROPE)
    nope, pe = uq[..., :QK_NOPE], uq[..., QK_NOPE:]
    rot = _rot_half(pe)
    zq = lambda n: jnp.zeros((Q_LORA, MLA_HEADS, n), F32)
    hw = MLA_HEADS * LANES
    wq_a = jnp.concatenate([nope, pe, zq(pad_hi)], axis=-1).reshape(Q_LORA, hw)
    wq_b = jnp.concatenate([zq(PE_LANE0), rot, zq(pad_hi)], axis=-1).reshape(Q_LORA, hw)
    wq_prompt = jnp.concatenate([wq_a, wq_b], axis=1).astype(BF16)
    wq_n = jnp.concatenate([nope, zq(LANES - QK_NOPE)], axis=-1).reshape(Q_LORA, hw)
    wq_sample = jnp.concatenate([wq_n, pe.reshape(Q_LORA, -1), rot.reshape(Q_LORA, -1)], axis=1).astype(BF16)

    ukv = w_ukv.reshape(KV_LORA, MLA_HEADS, QK_NOPE + V_HEAD)
    k_nope, v = ukv[..., :QK_NOPE], ukv[..., QK_NOPE:]
    zk = jnp.zeros((KV_LORA, MLA_HEADS, LANES - QK_NOPE), F32)
    wkv_prompt = jnp.concatenate([jnp.concatenate([k_nope, zk], axis=-1).reshape(KV_LORA, hw),
                                  v.reshape(KV_LORA, -1)], axis=1).astype(BF16)
    wuk_t = jnp.concatenate([k_nope, zk], axis=-1).transpose(1, 2, 0).astype(BF16)
    wuv = v.transpose(1, 0, 2).astype(BF16)
    return w_in_p, wq_prompt, wq_sample, wkv_prompt, wuk_t, wuv


def _rope_tables(pos):
    half = QK_ROPE // 2
    inv = ROPE_THETA ** (-jnp.arange(0, QK_ROPE, 2, dtype=F32) / QK_ROPE)
    ang = pos.astype(F32)[:, None] * inv[None, :]
    c2 = jnp.concatenate([jnp.cos(ang), jnp.cos(ang)], axis=1)
    s2 = jnp.concatenate([jnp.sin(ang), jnp.sin(ang)], axis=1)
    t = pos.shape[0]
    cos = jnp.concatenate([jnp.ones((t, PE_LANE0), F32), c2, jnp.zeros((t, LANES - PE_LANE0 - QK_ROPE), F32)], axis=1)
    sin = jnp.concatenate([jnp.zeros((t, PE_LANE0), F32), s2, jnp.zeros((t, LANES - PE_LANE0 - QK_ROPE), F32)], axis=1)
    return cos, sin, jnp.tile(c2, (1, MLA_HEADS)), jnp.tile(s2, (1, MLA_HEADS))


def _tile_rows(n, pref):
    t = min(pref, n)
    while n % t:
        t //= 2
    return t


def kernel(x_prompt, x_sample, cache_ckv, cache_kpe, state_ssm, state_conv, page_table, w_in, b_gate, q_norm_g, kv_norm_g, w_uq, w_ukv, conv_w, conv_b, dt_bias, a_log, d_skip, ssm_norm_g, w_branch_a, w_branch_b, w_out, ln1_g, ln1_b, w_ffn_gate, w_ffn_up, w_ffn_down, ln2_g, ln2_b):
    assert w_in.shape[0] == DEPTH == 1
    batch, seq, _ = x_prompt.shape
    dec_batch, t_new, _ = x_sample.shape
    n_pages = page_table.shape[1]
    past_len = n_pages * PAGE_SIZE
    n_p = batch * seq
    n_s = dec_batch * t_new

    w_in_p, wq_prompt, wq_sample, wkv_prompt, wuk_t, wuv = _prep_weights(w_in[0], w_uq[0], w_ukv[0])
    row2 = lambda v: v.reshape(1, -1).astype(F32)
    gq, gkv = row2(q_norm_g[0]), row2(kv_norm_g[0])
    pad_heads = lambda v: jnp.concatenate([v.astype(F32), jnp.zeros((LANES - SSM_HEADS,), F32)]).reshape(1, LANES)
    dtb, alog = pad_heads(dt_bias[0]), pad_heads(a_log[0])
    dskip = jnp.repeat(d_skip[0].astype(F32), SSM_HEADDIM).reshape(1, D_INNER)
    cw, cb = conv_w[0].astype(F32), row2(conv_b[0])
    bg, ng = row2(b_gate[0]), row2(ssm_norm_g[0])
    wa, wb, wo = w_branch_a[0].astype(BF16), w_branch_b[0].astype(BF16), w_out[0].astype(BF16)
    wfg, wfu, wfd = w_ffn_gate[0].astype(BF16), w_ffn_up[0].astype(BF16), w_ffn_down[0].astype(BF16)
    l1g, l1b, l2g, l2b = row2(ln1_g[0]), row2(ln1_b[0]), row2(ln2_g[0]), row2(ln2_b[0])

    xp = x_prompt.reshape(n_p, D_MODEL)
    proj_p = _matmul(xp, w_in_p, _tile_rows(n_p, 1024), 1024, F32)
    cos_p, sin_p, _, _ = _rope_tables(jnp.arange(seq))
    q_p, k_p, v_p, ckv_p, kpe_p = _mla_prep_prompt(proj_p, cos_p, sin_p, gq, gkv, wq_prompt, wkv_prompt,
                                                   _tile_rows(seq, 256))
    attn_p = _prompt_attention(q_p, k_p, v_p, batch, seq, _tile_rows(seq, 512))
    chunk_p = min(SSD_CHUNK, seq)
    y_p, ssm_p, conv_p = _ssd(proj_p.reshape(batch, seq, PROJ_COLS), None, None, cw, cb, dtb, alog, dskip,
                              chunk_p, chunk_p)
    h_p = _merge(xp, attn_p, y_p.reshape(n_p, D_INNER), proj_p, bg, ng, wa, wb, wo, l1g, l1b, _tile_rows(n_p, 256))
    out_p = _ffn(h_p, wfg, wfu, wfd, l2g, l2b, _tile_rows(n_p, 256))

    xs = x_sample.reshape(n_s, D_MODEL)
    proj_s = _matmul(xs, w_in_p, _tile_rows(n_s, 512), 1024, F32)
    cos_s, sin_s, cosp_s, sinp_s = _rope_tables(past_len + jnp.arange(t_new))
    tm_s = t_new * _tile_rows(dec_batch, 64)
    rep = lambda t: jnp.tile(t, (tm_s // t_new, 1))
    qlat_s, qpe_s, ckv_s, kpe_s = _mla_prep_sample(proj_s, rep(cos_s), rep(sin_s), rep(cosp_s), rep(sinp_s),
                                                   gq, gkv, wq_sample, wuk_t, tm_s)
    pps = _tile_rows(n_pages, 32)
    kpe_t = jnp.swapaxes(cache_kpe.reshape(cache_kpe.shape[1:]), 1, 2)
    olat_s = _paged_attention(page_table, qlat_s.reshape(n_s * MLA_HEADS, KV_LORA),
                              qpe_s.reshape(n_s * MLA_HEADS, QK_ROPE),
                              ckv_s.reshape(dec_batch, t_new, KV_LORA), kpe_s.reshape(dec_batch, t_new, QK_ROPE),
                              cache_ckv.reshape(cache_ckv.shape[1:]), kpe_t, t_new, pps)
    attn_s = _uv_proj(olat_s.reshape(n_s, MLA_HEADS * KV_LORA), wuv, _tile_rows(n_s, 256))
    conv0 = jnp.concatenate([jnp.zeros((dec_batch, SUBLANES - (CONV_WIDTH - 1), CONV_DIM), F32),
                             state_conv[0].astype(F32)], axis=1)
    ssm0 = state_ssm.astype(F32).reshape(dec_batch, SSM_HEADS * SSM_HEADDIM, SSM_STATE)
    y_s, ssm_s, conv_s = _ssd(proj_s.reshape(dec_batch, t_new, PROJ_COLS), conv0, ssm0, cw, cb, dtb, alog, dskip,
                              t_new, LANES)
    h_s = _merge(xs, attn_s, y_s.reshape(n_s, D_INNER), proj_s, bg, ng, wa, wb, wo, l1g, l1b, _tile_rows(n_s, 256))
    out_s = _ffn(h_s, wfg, wfu, wfd, l2g, l2b, _tile_rows(n_s, 256))

    st_shape = (SSM_HEADS, SSM_HEADDIM, SSM_STATE)
    return (out_p.reshape(batch, seq, D_MODEL), out_s.reshape(dec_batch, t_new, D_MODEL),
            ckv_p.reshape(1, batch, seq, KV_LORA), kpe_p.reshape(1, batch, seq, QK_ROPE),
            ssm_p.reshape(1, batch, *st_shape), conv_p.reshape(1, batch, CONV_WIDTH - 1, CONV_DIM),
            ckv_s.reshape(1, dec_batch, t_new, KV_LORA), kpe_s.reshape(1, dec_batch, t_new, QK_ROPE),
            ssm_s.reshape(1, dec_batch, *st_shape), conv_s.reshape(1, dec_batch, CONV_WIDTH - 1, CONV_DIM))
```

```python
import functools
import math

import jax
import jax.numpy as jnp
from jax import lax
from jax.experimental import pallas as pl
from jax.experimental.pallas import tpu as pltpu

F32 = jnp.float32
BF16 = jnp.bfloat16

D_MODEL = 1024
MLA_HEADS = 16
QK_NOPE = 64
QK_ROPE = 32
V_HEAD = 64
Q_LORA = 384
KV_LORA = 256
ROPE_THETA = 10000.0
ATTN_SCALE = (QK_NOPE + QK_ROPE) ** -0.5
Q_PRESCALE = ATTN_SCALE * math.log2(math.e)
D_INNER = 2 * D_MODEL
SSM_HEADDIM = 64
SSM_HEADS = D_INNER // SSM_HEADDIM
SSM_GROUPS = 4
HEADS_PER_GROUP = SSM_HEADS // SSM_GROUPS
SSM_STATE = 128
CONV_WIDTH = 4
CONV_DIM = D_INNER + 2 * SSM_GROUPS * SSM_STATE
SSD_CHUNK = 256
D_FF = ((8 * D_MODEL // 3 + 255) // 256) * 256
PAGE_SIZE = 128
DEPTH = 1
ALPHA = (2.0 * DEPTH) ** 0.25
IN_SPLITS = (Q_LORA, KV_LORA, QK_ROPE, D_INNER, CONV_DIM, SSM_HEADS, 2 * D_MODEL)
IN_OFFSETS = tuple(sum(IN_SPLITS[:i + 1]) for i in range(len(IN_SPLITS) - 1))

LANES = 128
SUBLANES = 8
VMEM_LIMIT_BYTES = 56 * 1024 * 1024

PROJ_XBC = 0
PROJ_SMALL = CONV_DIM
PROJ_Z = 4096
PROJ_G = 6144
PROJ_COLS = 8192
SMALL_CQ = 0
SMALL_CKV = Q_LORA
SMALL_KPE = Q_LORA + KV_LORA
SMALL_KROT = SMALL_KPE + LANES
SMALL_DT = SMALL_KROT + LANES
SMALL_COLS = 1024
PE_LANE0 = QK_NOPE


def _cparams(*sem):
    return pltpu.CompilerParams(dimension_semantics=sem, vmem_limit_bytes=VMEM_LIMIT_BYTES)


def _nt_dot(a, b):
    return lax.dot_general(a, b, (((1,), (1,)), ((), ())), preferred_element_type=F32)


def _silu(x):
    return x * (1.0 / (1.0 + jnp.exp(-x)))


def _sigmoid(x):
    return 1.0 / (1.0 + jnp.exp(-x))


def _softplus(x):
    return jnp.maximum(x, 0.0) + jnp.log(1.0 + jnp.exp(-jnp.abs(x)))


def _matmul_kernel(x_ref, w_ref, o_ref):
    o_ref[...] = jnp.dot(x_ref[...].astype(BF16), w_ref[...],
                         preferred_element_type=F32).astype(o_ref.dtype)


def _matmul(x, w, tm, tn, out_dtype):
    m, k = x.shape
    n = w.shape[1]
    return pl.pallas_call(
        _matmul_kernel,
        grid=(m // tm, n // tn),
        in_specs=[pl.BlockSpec((tm, k), lambda i, j: (i, 0)),
                  pl.BlockSpec((k, tn), lambda i, j: (0, j))],
        out_specs=pl.BlockSpec((tm, tn), lambda i, j: (i, j)),
        out_shape=jax.ShapeDtypeStruct((m, n), out_dtype),
        compiler_params=_cparams("parallel", "arbitrary"),
        name="in_proj",
    )(x, w)


def _rms(x, g, eps):
    return x * lax.rsqrt(jnp.mean(x * x, axis=-1, keepdims=True) + eps) * g


def _mla_common(blk, cos, sin, gq_ref, gkv_ref, ckv_ref, kpe_ref):
    cqn = _rms(blk[:, SMALL_CQ:SMALL_CQ + Q_LORA], gq_ref[...], 1e-6).astype(BF16)
    ckvn = _rms(blk[:, SMALL_CKV:SMALL_CKV + KV_LORA], gkv_ref[...], 1e-6)
    ckv_ref[...] = ckvn
    kr = blk[:, SMALL_KPE:SMALL_KPE + LANES] * cos + blk[:, SMALL_KROT:SMALL_KROT + LANES] * sin
    kpe_ref[...] = kr[:, PE_LANE0:PE_LANE0 + QK_ROPE]
    return cqn, ckvn, kr


def _mla_prep_prompt_kernel(blk_ref, cos_ref, sin_ref, gq_ref, gkv_ref, wq_ref, wkv_ref,
                            q_ref, k_ref, v_ref, ckv_ref, kpe_ref):
    cos = cos_ref[...]
    sin = sin_ref[...]
    cqn, ckvn, kr = _mla_common(blk_ref[...], cos, sin, gq_ref, gkv_ref, ckv_ref, kpe_ref)
    hw = MLA_HEADS * LANES
    qab = jnp.dot(cqn, wq_ref[...], preferred_element_type=F32)
    kv = jnp.dot(ckvn.astype(BF16), wkv_ref[...], preferred_element_type=F32)
    for h in range(MLA_HEADS):
        sl = slice(h * LANES, (h + 1) * LANES)
        q_ref[:, sl] = ((qab[:, sl] * cos + qab[:, hw + h * LANES:hw + (h + 1) * LANES] * sin)
                        * Q_PRESCALE).astype(BF16)
        k_ref[:, sl] = (kv[:, sl] + kr).astype(BF16)
    v_ref[...] = kv[:, hw:].astype(BF16)


def _mla_prep_sample_kernel(blk_ref, cos_ref, sin_ref, cosp_ref, sinp_ref, gq_ref, gkv_ref,
                            wq_ref, wuk_ref, qlat_ref, qpe_ref, ckv_ref, kpe_ref):
    cqn, _, _ = _mla_common(blk_ref[...], cos_ref[...], sin_ref[...], gq_ref, gkv_ref, ckv_ref, kpe_ref)
    hw = MLA_HEADS * LANES
    pw = MLA_HEADS * QK_ROPE
    qall = jnp.dot(cqn, wq_ref[...], preferred_element_type=F32)
    qpe_ref[...] = ((qall[:, hw:hw + pw] * cosp_ref[...] + qall[:, hw + pw:] * sinp_ref[...])
                    * Q_PRESCALE).astype(BF16)
    for h in range(MLA_HEADS):
        qn = qall[:, h * LANES:(h + 1) * LANES].astype(BF16)
        qlat_ref[:, h * KV_LORA:(h + 1) * KV_LORA] = (jnp.dot(
            qn, wuk_ref[h], preferred_element_type=F32) * Q_PRESCALE).astype(BF16)


def _const_spec(shape):
    nd = len(shape)
    return pl.BlockSpec(shape, lambda *_: (0,) * nd)


def _mla_prep_prompt(proj, cos, sin, gq, gkv, wq, wkv, tm):
    n = proj.shape[0]
    hw = MLA_HEADS * LANES
    small_blk = PROJ_SMALL // SMALL_COLS
    row = lambda w: pl.BlockSpec((tm, w), lambda i: (i, 0))
    period = cos.shape[0] // tm
    tab = lambda w: pl.BlockSpec((tm, w), lambda i: (i % period, 0))
    return pl.pallas_call(
        _mla_prep_prompt_kernel,
        grid=(n // tm,),
        in_specs=[pl.BlockSpec((tm, SMALL_COLS), lambda i: (i, small_blk)),
                  tab(LANES), tab(LANES),
                  _const_spec(gq.shape), _const_spec(gkv.shape),
                  _const_spec(wq.shape), _const_spec(wkv.shape)],
        out_specs=[row(hw), row(hw), row(MLA_HEADS * V_HEAD), row(KV_LORA), row(QK_ROPE)],
        out_shape=[jax.ShapeDtypeStruct((n, hw), BF16), jax.ShapeDtypeStruct((n, hw), BF16),
                   jax.ShapeDtypeStruct((n, MLA_HEADS * V_HEAD), BF16),
                   jax.ShapeDtypeStruct((n, KV_LORA), F32), jax.ShapeDtypeStruct((n, QK_ROPE), F32)],
        compiler_params=_cparams("parallel"),
        name="mla_prep_prompt",
    )(proj, cos, sin, gq, gkv, wq, wkv)


def _mla_prep_sample(proj, cos, sin, cosp, sinp, gq, gkv, wq, wuk, tm):
    n = proj.shape[0]
    pw = MLA_HEADS * QK_ROPE
    small_blk = PROJ_SMALL // SMALL_COLS
    row = lambda w: pl.BlockSpec((tm, w), lambda i: (i, 0))
    tab = lambda w: pl.BlockSpec((tm, w), lambda i: (0, 0))
    return pl.pallas_call(
        _mla_prep_sample_kernel,
        grid=(n // tm,),
        in_specs=[pl.BlockSpec((tm, SMALL_COLS), lambda i: (i, small_blk)),
                  tab(LANES), tab(LANES), tab(pw), tab(pw),
                  _const_spec(gq.shape), _const_spec(gkv.shape),
                  _const_spec(wq.shape), _const_spec(wuk.shape)],
        out_specs=[row(MLA_HEADS * KV_LORA), row(pw), row(KV_LORA), row(QK_ROPE)],
        out_shape=[jax.ShapeDtypeStruct((n, MLA_HEADS * KV_LORA), BF16),
                   jax.ShapeDtypeStruct((n, pw), BF16),
                   jax.ShapeDtypeStruct((n, KV_LORA), F32), jax.ShapeDtypeStruct((n, QK_ROPE), F32)],
        compiler_params=_cparams("parallel"),
        name="mla_prep_sample",
    )(proj, cos, sin, cosp, sinp, gq, gkv, wq, wuk)


def _flash_kernel(q_ref, k_ref, v_ref, o_ref, *, tq):
    qi = pl.program_id(2)
    q2 = q_ref[...]
    qs = (q2[:, :LANES], q2[:, LANES:])
    neg = jnp.finfo(F32).min

    def tile(j, carry, masked):
        start = pl.multiple_of(j * tq, tq)
        v = v_ref[pl.ds(start, tq), :]
        new = []
        for hh in range(2):
            m, l, acc = carry[hh]
            k = k_ref[pl.ds(start, tq), hh * LANES:(hh + 1) * LANES]
            s = _nt_dot(qs[hh], k)
            if masked:
                row = lax.broadcasted_iota(jnp.int32, (tq, tq), 0)
                col = lax.broadcasted_iota(jnp.int32, (tq, tq), 1)
                s = jnp.where(col <= row, s, neg)
            m_new = jnp.maximum(m, jnp.max(s, axis=-1, keepdims=True))
            alpha = jnp.exp2(m - m_new)
            p = jnp.exp2(s - m_new)
            l = alpha * l + jnp.sum(p, axis=-1, keepdims=True)
            acc = alpha * acc + jnp.dot(p.astype(BF16), v, preferred_element_type=F32)
            new.append((m_new, l, acc))
        return tuple(new)

    one = (jnp.full((tq, 1), -jnp.inf, F32), jnp.zeros((tq, 1), F32), jnp.zeros((tq, LANES), F32))
    carry = lax.fori_loop(0, qi, functools.partial(tile, masked=False), (one, one))
    (_, l0, acc0), (_, l1, acc1) = tile(qi, carry, True)
    lane = lax.broadcasted_iota(jnp.int32, (tq, LANES), 1)
    o_ref[...] = jnp.where(lane < V_HEAD, acc0 / l0, acc1 / l1).astype(o_ref.dtype)


def _prompt_attention(q, k, v, batch, seq, tq):
    n = batch * seq
    nq = seq // tq
    pairs = MLA_HEADS // 2
    return pl.pallas_call(
        functools.partial(_flash_kernel, tq=tq),
        grid=(batch, pairs, nq),
        in_specs=[pl.BlockSpec((tq, 2 * LANES), lambda b, p, i: (b * nq + i, p)),
                  pl.BlockSpec((seq, 2 * LANES), lambda b, p, i: (b, p)),
                  pl.BlockSpec((seq, 2 * V_HEAD), lambda b, p, i: (b, p))],
        out_specs=pl.BlockSpec((tq, 2 * V_HEAD), lambda b, p, i: (b * nq + i, p)),
        out_shape=jax.ShapeDtypeStruct((n, MLA_HEADS * V_HEAD), BF16),
        compiler_params=_cparams("parallel", "parallel", "arbitrary"),
        name="prompt_attention",
    )(q, k, v)


def _paged_attn_kernel(pt_ref, qlat_ref, qpe_ref, ckvn_ref, kpen_ref, ckv_hbm, kpe_hbm, o_ref,
                       ckv_buf, kpe_buf, sem, m_sc, l_sc, acc_sc, *, pps, sub_pages, t_new, n_chunks):
    step = pl.program_id(0)
    n_steps = pl.num_programs(0)
    c = step % n_chunks
    slot = step % 2
    rows = t_new * MLA_HEADS

    def page_copies(s, sl):
        out = []
        for k in range(pps):
            page = pt_ref[s * pps + k]
            tok = pl.ds(k * PAGE_SIZE, PAGE_SIZE)
            out.append(pltpu.make_async_copy(ckv_hbm.at[page], ckv_buf.at[sl, tok, :], sem.at[sl]))
            out.append(pltpu.make_async_copy(kpe_hbm.at[page], kpe_buf.at[sl, :, tok], sem.at[sl]))
        return out

    @pl.when(step == 0)
    def _():
        for cp in page_copies(step, slot):
            cp.start()

    @pl.when(c == 0)
    def _():
        m_sc[...] = jnp.full(m_sc.shape, -jnp.inf, F32)
        l_sc[...] = jnp.zeros(l_sc.shape, F32)
        acc_sc[...] = jnp.zeros(acc_sc.shape, F32)

    for cp in page_copies(step, slot):
        cp.wait()

    nxt = jnp.where(step + 1 < n_steps, step + 1, 0)
    for cp in page_copies(nxt, 1 - slot):
        cp.start()

    qlat = qlat_ref[...]
    qpe = qpe_ref[...]
    m_run, l_run, acc = m_sc[...], l_sc[...], acc_sc[...]
    for j in range(pps // sub_pages):
        tok = pl.ds(j * sub_pages * PAGE_SIZE, sub_pages * PAGE_SIZE)
        ck = ckv_buf[slot, tok, :].astype(BF16)
        kp = kpe_buf[slot, :, tok].astype(BF16)
        s = _nt_dot(qlat, ck) + jnp.dot(qpe, kp, preferred_element_type=F32)
        m_new = jnp.maximum(m_run, jnp.max(s, axis=-1, keepdims=True))
        alpha = jnp.exp2(m_run - m_new)
        p = jnp.exp2(s - m_new)
        l_run = alpha * l_run + jnp.sum(p, axis=-1, keepdims=True)
        acc = alpha * acc + jnp.dot(p.astype(BF16), ck, preferred_element_type=F32)
        m_run = m_new
    m_sc[...], l_sc[...], acc_sc[...] = m_run, l_run, acc

    @pl.when(c == n_chunks - 1)
    def _():
        qlf = qlat.astype(F32)
        qpf = qpe.astype(F32)
        cn = ckvn_ref[...].astype(BF16).astype(F32)
        kn = kpen_ref[...].astype(BF16).astype(F32)
        tok = lax.broadcasted_iota(jnp.int32, (rows, 1), 0) // MLA_HEADS
        neg = jnp.finfo(F32).min
        s_new = []
        for j in range(t_new):
            sj = (jnp.sum(qlf * cn[j:j + 1, :], axis=-1, keepdims=True)
                  + jnp.sum(qpf * kn[j:j + 1, :], axis=-1, keepdims=True))
            s_new.append(jnp.where(tok >= j, sj, neg))
        m1 = m_sc[...]
        m2 = m1
        for sj in s_new:
            m2 = jnp.maximum(m2, sj)
        a2 = jnp.exp2(m1 - m2)
        l2 = a2 * l_sc[...]
        acc2 = a2 * acc_sc[...]
        for j, sj in enumerate(s_new):
            pj = jnp.exp2(sj - m2)
            l2 = l2 + pj
            acc2 = acc2 + pj.astype(BF16).astype(F32) * cn[j:j + 1, :]
        o_ref[...] = (acc2 / l2).astype(o_ref.dtype)

    @pl.when(step == n_steps - 1)
    def _():
        for cp in page_copies(nxt, 1 - slot):
            cp.wait()


def _paged_attention(page_table, qlat, qpe, ckv_new, kpe_new, cache_ckv, cache_kpe, t_new, pps):
    dec_batch, n_pages = page_table.shape
    rows = t_new * MLA_HEADS
    n_chunks = n_pages // pps
    by_seq = lambda s, pt: (s // n_chunks, 0)
    by_seq3 = lambda s, pt: (s // n_chunks, 0, 0)
    grid_spec = pltpu.PrefetchScalarGridSpec(
        num_scalar_prefetch=1,
        grid=(dec_batch * n_chunks,),
        in_specs=[pl.BlockSpec((rows, KV_LORA), by_seq),
                  pl.BlockSpec((rows, QK_ROPE), by_seq),
                  pl.BlockSpec((None, t_new, KV_LORA), by_seq3),
                  pl.BlockSpec((None, t_new, QK_ROPE), by_seq3),
                  pl.BlockSpec(memory_space=pl.ANY),
                  pl.BlockSpec(memory_space=pl.ANY)],
        out_specs=pl.BlockSpec((rows, KV_LORA), by_seq),
        scratch_shapes=[pltpu.VMEM((2, pps * PAGE_SIZE, KV_LORA), F32),
                        pltpu.VMEM((2, QK_ROPE, pps * PAGE_SIZE), F32),
                        pltpu.SemaphoreType.DMA((2,)),
                        pltpu.VMEM((rows, 1), F32), pltpu.VMEM((rows, 1), F32),
                        pltpu.VMEM((rows, KV_LORA), F32)],
    )
    return pl.pallas_call(
        functools.partial(_paged_attn_kernel, pps=pps, sub_pages=_tile_rows(pps, 16), t_new=t_new,
                          n_chunks=n_chunks),
        grid_spec=grid_spec,
        out_shape=jax.ShapeDtypeStruct((dec_batch * rows, KV_LORA), BF16),
        compiler_params=_cparams("arbitrary"),
        name="paged_attention",
    )(page_table.reshape(-1), qlat, qpe, ckv_new, kpe_new, cache_ckv, cache_kpe)


def _uv_proj_kernel(olat_ref, wuv_ref, o_ref):
    for h in range(MLA_HEADS):
        o_ref[:, h * V_HEAD:(h + 1) * V_HEAD] = jnp.dot(
            olat_ref[:, h * KV_LORA:(h + 1) * KV_LORA], wuv_ref[h],
            preferred_element_type=F32).astype(o_ref.dtype)


def _uv_proj(olat, wuv, tm):
    n = olat.shape[0]
    return pl.pallas_call(
        _uv_proj_kernel,
        grid=(n // tm,),
        in_specs=[pl.BlockSpec((tm, MLA_HEADS * KV_LORA), lambda i: (i, 0)), _const_spec(wuv.shape)],
        out_specs=pl.BlockSpec((tm, MLA_HEADS * V_HEAD), lambda i: (i, 0)),
        out_shape=jax.ShapeDtypeStruct((n, MLA_HEADS * V_HEAD), BF16),
        compiler_params=_cparams("parallel"),
        name="uv_proj",
    )(olat, wuv)


def _split3(x):
    hi = x.astype(BF16)
    r = x - hi.astype(F32)
    mid = r.astype(BF16)
    lo = (r - mid.astype(F32)).astype(BF16)
    return hi, mid, lo


def _expand_heads(v, expand_ref):
    hi = v.astype(BF16)
    mid = (v - hi.astype(F32)).astype(BF16)
    return jnp.dot(jnp.concatenate([hi, mid], axis=1), expand_ref[...], preferred_element_type=F32)


def _ssd_heads_wide(xc, dt, a_cum, a_cum_t, exp_cum, dte, tril, dskip_ref, expand_ref, y_ref, state_t):
    chunk = xc.shape[0]
    gw = HEADS_PER_GROUP * SSM_HEADDIM
    b_off = D_INNER
    c_off = D_INNER + SSM_GROUPS * SSM_STATE
    xs = xc[:, :D_INNER]
    dt_e = _expand_heads(dt, expand_ref)
    exp_cum_e = _expand_heads(exp_cum, expand_ref)
    dte_e = _expand_heads(dte, expand_ref)
    xdt = xs * dt_e
    xdt_b = xdt.astype(BF16)
    xdte_b = (xdt * dte_e).astype(BF16)
    skip = xs * dskip_ref[...]
    lane = lax.broadcasted_iota(jnp.int32, (chunk, LANES), 1)
    for g in range(SSM_GROUPS):
        gcols = slice(g * gw, (g + 1) * gw)
        bg = xc[:, b_off + g * SSM_STATE:b_off + (g + 1) * SSM_STATE].astype(BF16)
        cg = xc[:, c_off + g * SSM_STATE:c_off + (g + 1) * SSM_STATE].astype(BF16)
        cb = _nt_dot(cg, bg)
        st_g = state_t[:, gcols]
        y_off = jnp.dot(cg, st_g.astype(BF16), preferred_element_type=F32)
        for pair in range(HEADS_PER_GROUP // 2):
            h0 = g * HEADS_PER_GROUP + 2 * pair
            cols = slice(h0 * SSM_HEADDIM, (h0 + 2) * SSM_HEADDIM)
            rhs = xdt_b[:, cols]
            ys = []
            for h in (h0, h0 + 1):
                seg = a_cum[:, h:h + 1] - a_cum_t[h:h + 1, :]
                decay = jnp.exp(jnp.where(tril, seg, -jnp.inf))
                ys.append(jnp.dot((cb * decay).astype(BF16), rhs, preferred_element_type=F32))
            y = jnp.where(lane < SSM_HEADDIM, ys[0], ys[1])
            y_ref[:, cols] = (y + y_off[:, pair * LANES:(pair + 1) * LANES] * exp_cum_e[:, cols]
                              + skip[:, cols])
        cs_t = lax.dot_general(bg, xdte_b[:, gcols], (((0,), (0,)), ((), ())),
                               preferred_element_type=F32)
        state_t[:, gcols] = st_g * exp_cum_e[chunk - 1:chunk, gcols] + cs_t


def _ssd_kernel(*refs, l_in, chunk, has_init):
    if has_init:
        (xbc_ref, dt_ref, conv0_ref, ssm0_ref, cw_ref, cb_ref, dtb_ref, alog_ref, dskip_ref,
         y_ref, ssm_ref, conv_ref, xpad, state, dtpad) = refs
    else:
        (xbc_ref, dt_ref, cw_ref, cb_ref, dtb_ref, alog_ref, dskip_ref, expand_ref,
         y_ref, ssm_ref, conv_ref, xpad, state, dtpad) = refs
    c = pl.program_id(1)
    last = pl.num_programs(1) - 1
    tail = SUBLANES

    @pl.when(c == 0)
    def _():
        if has_init:
            xpad[0:tail, :] = conv0_ref[...]
            state[...] = ssm0_ref[...]
        else:
            xpad[0:tail, :] = jnp.zeros((tail, CONV_DIM), F32)
            state[...] = jnp.zeros(state.shape, F32)
        if l_in < chunk:
            xpad[tail:, :] = jnp.zeros((chunk, CONV_DIM), F32)

    xpad[tail:tail + l_in, :] = xbc_ref[...]

    conv = cb_ref[...] + cw_ref[CONV_WIDTH - 1:CONV_WIDTH, :] * xpad[tail:tail + chunk, :]
    for k in range(CONV_WIDTH - 1):
        off = tail - (CONV_WIDTH - 1) + k
        conv = conv + cw_ref[k:k + 1, :] * xpad[off:off + chunk, :]
    xc = _silu(conv)

    win = tail + ((l_in - (CONV_WIDTH - 1)) // SUBLANES) * SUBLANES
    sub = (l_in - (CONV_WIDTH - 1)) % SUBLANES

    @pl.when(c == last)
    def _():
        conv_ref[...] = xpad[win:win + SUBLANES, :][sub:sub + CONV_WIDTH - 1, :]

    xpad[0:tail, :] = xpad[chunk:chunk + tail, :]

    rows = lax.broadcasted_iota(jnp.int32, (chunk, LANES), 0)
    if l_in < chunk:
        dtpad[...] = jnp.zeros((chunk, LANES), F32)
        dtpad[0:l_in, :] = dt_ref[...]
        dt_raw = dtpad[...]
    else:
        dt_raw = dt_ref[...]
    dt = jnp.where(rows < l_in, _softplus(dt_raw + dtb_ref[...]), 0.0)
    a = -jnp.exp(alog_ref[...])
    da = dt * a

    ti = lax.broadcasted_iota(jnp.int32, (chunk, chunk), 0)
    si = lax.broadcasted_iota(jnp.int32, (chunk, chunk), 1)
    tril = ti >= si
    tril_b = tril.astype(BF16)
    hi, mid, lo = _split3(da)
    cum3 = jnp.dot(tril_b, jnp.concatenate([hi, mid, lo], axis=1), preferred_element_type=F32)
    a_cum = (cum3[:, 0:LANES] + cum3[:, LANES:2 * LANES]) + cum3[:, 2 * LANES:3 * LANES]
    a_cum_t = a_cum.T
    exp_cum = jnp.exp(a_cum)
    a_last = a_cum[chunk - 1:chunk, :]
    dte = jnp.exp(a_last - a_cum)

    gw = HEADS_PER_GROUP * SSM_HEADDIM
    b_off = D_INNER
    c_off = D_INNER + SSM_GROUPS * SSM_STATE

    if not has_init:
        assert l_in == chunk
        _ssd_heads_wide(xc, dt, a_cum, a_cum_t, exp_cum, dte, tril, dskip_ref, expand_ref, y_ref, state)

        @pl.when(c == last)
        def _():
            ssm_ref[...] = state[...].T
        return

    chunk_decay_t = jnp.exp(a_cum_t[:, chunk - 1:chunk])
    for g in range(SSM_GROUPS):
        bg = xc[:, b_off + g * SSM_STATE:b_off + (g + 1) * SSM_STATE].astype(BF16)
        cg = xc[:, c_off + g * SSM_STATE:c_off + (g + 1) * SSM_STATE].astype(BF16)
        cb = _nt_dot(cg, bg)
        st_g = state[g * gw:(g + 1) * gw, :]
        y_off = _nt_dot(cg, st_g.astype(BF16))
        xdte_parts = []
        for hl in range(HEADS_PER_GROUP):
            h = g * HEADS_PER_GROUP + hl
            xs = xc[:, h * SSM_HEADDIM:(h + 1) * SSM_HEADDIM]
            xdt = xs * dt[:, h:h + 1]
            seg = a_cum[:, h:h + 1] - a_cum_t[h:h + 1, :]
            decay = jnp.exp(jnp.where(tril, seg, -jnp.inf))
            y = jnp.dot((cb * decay).astype(BF16), xdt.astype(BF16), preferred_element_type=F32)
            y = y + y_off[:, hl * SSM_HEADDIM:(hl + 1) * SSM_HEADDIM] * exp_cum[:, h:h + 1]
            y = y + xs * dskip_ref[:, h * SSM_HEADDIM:(h + 1) * SSM_HEADDIM]
            y_ref[:, h * SSM_HEADDIM:(h + 1) * SSM_HEADDIM] = y[0:l_in, :]
            xdte_parts.append(xdt * dte[:, h:h + 1])
        xdte_t = jnp.concatenate(xdte_parts, axis=1).T.astype(BF16)
        cs = jnp.dot(xdte_t, bg, preferred_element_type=F32)
        for hl in range(HEADS_PER_GROUP):
            h = g * HEADS_PER_GROUP + hl
            r0 = g * gw + hl * SSM_HEADDIM
            state[r0:r0 + SSM_HEADDIM, :] = (st_g[hl * SSM_HEADDIM:(hl + 1) * SSM_HEADDIM, :]
                                             * chunk_decay_t[h:h + 1, :]
                                             + cs[hl * SSM_HEADDIM:(hl + 1) * SSM_HEADDIM, :])

    @pl.when(c == last)
    def _():
        ssm_ref[...] = state[...]


def _ssd(proj3, conv0, ssm0, cw, cb, dtb, alog, dskip, l_in, chunk):
    batch, seq, _ = proj3.shape
    nc = seq // l_in
    assert l_in == chunk or nc == 1
    has_init = conv0 is not None
    hp = SSM_HEADS * SSM_HEADDIM
    dt_blk = (PROJ_SMALL + SMALL_DT) // LANES
    in_specs = [pl.BlockSpec((None, l_in, CONV_DIM), lambda b, c: (b, c, PROJ_XBC // CONV_DIM)),
                pl.BlockSpec((None, l_in, LANES), lambda b, c: (b, c, dt_blk))]
    args = [proj3, proj3]
    if has_init:
        in_specs += [pl.BlockSpec((None, SUBLANES, CONV_DIM), lambda b, c: (b, 0, 0)),
                     pl.BlockSpec((None, hp, SSM_STATE), lambda b, c: (b, 0, 0))]
        args += [conv0, ssm0]
    consts = [cw, cb, dtb, alog, dskip]
    if not has_init:
        sel = (jnp.arange(LANES)[:, None] == jnp.arange(hp)[None, :] // SSM_HEADDIM).astype(BF16)
        consts.append(jnp.concatenate([sel, sel], axis=0))
    state_shape = (hp, SSM_STATE) if has_init else (SSM_STATE, hp)
    in_specs += [_const_spec(x.shape) for x in consts]
    args += consts
    return pl.pallas_call(
        functools.partial(_ssd_kernel, l_in=l_in, chunk=chunk, has_init=has_init),
        grid=(batch, nc),
        in_specs=in_specs,
        out_specs=[pl.BlockSpec((None, l_in, D_INNER), lambda b, c: (b, c, 0)),
                   pl.BlockSpec((None, hp, SSM_STATE), lambda b, c: (b, 0, 0)),
                   pl.BlockSpec((None, CONV_WIDTH - 1, CONV_DIM), lambda b, c: (b, 0, 0))],
        out_shape=[jax.ShapeDtypeStruct((batch, seq, D_INNER), F32),
                   jax.ShapeDtypeStruct((batch, hp, SSM_STATE), F32),
                   jax.ShapeDtypeStruct((batch, CONV_WIDTH - 1, CONV_DIM), F32)],
        scratch_shapes=[pltpu.VMEM((SUBLANES + chunk, CONV_DIM), F32), pltpu.VMEM(state_shape, F32),
                        pltpu.VMEM((chunk, LANES), F32)],
        compiler_params=_cparams("parallel", "arbitrary"),
        name="conv_ssd",
    )(*args)


def _layer_norm(x, g, b):
    mu = jnp.mean(x, axis=-1, keepdims=True)
    xc = x - mu
    var = jnp.mean(xc * xc, axis=-1, keepdims=True)
    return xc * lax.rsqrt(var + 1e-5) * g + b


def _merge_kernel(x_ref, attn_ref, y_ref, z_ref, g_ref, bg_ref, ng_ref, wa_ref, wb_ref, wo_ref,
                  l1g_ref, l1b_ref, h_ref):
    gsz = D_INNER // SSM_GROUPS
    yg = y_ref[...] * _silu(z_ref[...])
    parts = []
    for g in range(SSM_GROUPS):
        parts.append(_rms(yg[:, g * gsz:(g + 1) * gsz], ng_ref[:, g * gsz:(g + 1) * gsz], 1e-5).astype(BF16))
    ygn = jnp.concatenate(parts, axis=1)
    br_a = jnp.dot(attn_ref[...], wa_ref[...], preferred_element_type=F32)
    br_b = jnp.dot(ygn, wb_ref[...], preferred_element_type=F32)
    gates = _sigmoid(g_ref[...] + bg_ref[...])
    merged = gates[:, :D_MODEL] * br_a + gates[:, D_MODEL:] * br_b
    mix = jnp.dot(merged.astype(BF16), wo_ref[...], preferred_element_type=F32)
    h_ref[...] = _layer_norm(ALPHA * x_ref[...] + mix, l1g_ref[...], l1b_ref[...])


def _merge(x, attn, y, proj, bg, ng, wa, wb, wo, l1g, l1b, tm):
    n = x.shape[0]
    row = lambda w: pl.BlockSpec((tm, w), lambda i: (i, 0))
    consts = [bg, ng, wa, wb, wo, l1g, l1b]
    return pl.pallas_call(
        _merge_kernel,
        grid=(n // tm,),
        in_specs=[row(D_MODEL), row(MLA_HEADS * V_HEAD), row(D_INNER),
                  pl.BlockSpec((tm, D_INNER), lambda i: (i, PROJ_Z // D_INNER)),
                  pl.BlockSpec((tm, 2 * D_MODEL), lambda i: (i, PROJ_G // (2 * D_MODEL)))]
                 + [_const_spec(c.shape) for c in consts],
        out_specs=row(D_MODEL),
        out_shape=jax.ShapeDtypeStruct((n, D_MODEL), F32),
        compiler_params=_cparams("parallel"),
        name="merge_ln1",
    )(x, attn, y, proj, proj, *consts)


def _ffn_kernel(h_ref, wg_ref, wu_ref, wd_ref, l2g_ref, l2b_ref, o_ref):
    h = h_ref[...]
    hb = h.astype(BF16)
    g = jnp.dot(hb, wg_ref[...], preferred_element_type=F32)
    u = jnp.dot(hb, wu_ref[...], preferred_element_type=F32)
    f = jnp.dot((_silu(g) * u).astype(BF16), wd_ref[...], preferred_element_type=F32)
    o_ref[...] = _layer_norm(ALPHA * h + f, l2g_ref[...], l2b_ref[...])


def _ffn(h, wg, wu, wd, l2g, l2b, tm):
    n = h.shape[0]
    row = pl.BlockSpec((tm, D_MODEL), lambda i: (i, 0))
    consts = [wg, wu, wd, l2g, l2b]
    return pl.pallas_call(
        _ffn_kernel,
        grid=(n // tm,),
        in_specs=[row] + [_const_spec(c.shape) for c in consts],
        out_specs=row,
        out_shape=jax.ShapeDtypeStruct((n, D_MODEL), F32),
        compiler_params=_cparams("parallel"),
        name="ffn_ln2",
    )(h, *consts)


def _rot_half(w):
    half = QK_ROPE // 2
    return jnp.concatenate([-w[..., half:], w[..., :half]], axis=-1)


def _prep_weights(w_in, w_uq, w_ukv):
    wq, wkv, wkpe, wz, wxbc, wdt, wg = jnp.split(w_in, IN_OFFSETS, axis=1)
    zc = lambda n: jnp.zeros((D_MODEL, n), F32)
    pad_hi = LANES - PE_LANE0 - QK_ROPE
    small = jnp.concatenate([wq, wkv, zc(PE_LANE0), wkpe, zc(pad_hi), zc(PE_LANE0), _rot_half(wkpe), zc(pad_hi),
                             wdt, zc(LANES - SSM_HEADS)], axis=1)
    w_in_p = jnp.concatenate([wxbc, small, wz, wg], axis=1).astype(BF16)

    uq = w_uq.reshape(Q_LORA, MLA_HEADS, QK_NOPE + QK_ROPE)
    nope, pe = uq[..., :QK_NOPE], uq[..., QK_NOPE:]
    rot = _rot_half(pe)
    zq = lambda n: jnp.zeros((Q_LORA, MLA_HEADS, n), F32)
    hw = MLA_HEADS * LANES
    wq_a = jnp.concatenate([nope, pe, zq(pad_hi)], axis=-1).reshape(Q_LORA, hw)
    wq_b = jnp.concatenate([zq(PE_LANE0), rot, zq(pad_hi)], axis=-1).reshape(Q_LORA, hw)
    wq_prompt = jnp.concatenate([wq_a, wq_b], axis=1).astype(BF16)
    wq_n = jnp.concatenate([nope, zq(LANES - QK_NOPE)], axis=-1).reshape(Q_LORA, hw)
    wq_sample = jnp.concatenate([wq_n, pe.reshape(Q_LORA, -1), rot.reshape(Q_LORA, -1)], axis=1).astype(BF16)

    ukv = w_ukv.reshape(KV_LORA, MLA_HEADS, QK_NOPE + V_HEAD)
    k_nope, v = ukv[..., :QK_NOPE], ukv[..., QK_NOPE:]
    zk = jnp.zeros((KV_LORA, MLA_HEADS, LANES - QK_NOPE), F32)
    wkv_prompt = jnp.concatenate([jnp.concatenate([k_nope, zk], axis=-1).reshape(KV_LORA, hw),
                                  v.reshape(KV_LORA, -1)], axis=1).astype(BF16)
    wuk_t = jnp.concatenate([k_nope, zk], axis=-1).transpose(1, 2, 0).astype(BF16)
    wuv = v.transpose(1, 0, 2).astype(BF16)
    return w_in_p, wq_prompt, wq_sample, wkv_prompt, wuk_t, wuv


def _rope_tables(pos):
    half = QK_ROPE // 2
    inv = ROPE_THETA ** (-jnp.arange(0, QK_ROPE, 2, dtype=F32) / QK_ROPE)
    ang = pos.astype(F32)[:, None] * inv[None, :]
    c2 = jnp.concatenate([jnp.cos(ang), jnp.cos(ang)], axis=1)
    s2 = jnp.concatenate([jnp.sin(ang), jnp.sin(ang)], axis=1)
    t = pos.shape[0]
    cos = jnp.concatenate([jnp.ones((t, PE_LANE0), F32), c2, jnp.zeros((t, LANES - PE_LANE0 - QK_ROPE), F32)], axis=1)
    sin = jnp.concatenate([jnp.zeros((t, PE_LANE0), F32), s2, jnp.zeros((t, LANES - PE_LANE0 - QK_ROPE), F32)], axis=1)
    return cos, sin, jnp.tile(c2, (1, MLA_HEADS)), jnp.tile(s2, (1, MLA_HEADS))


def _tile_rows(n, pref):
    t = min(pref, n)
    while n % t:
        t //= 2
    return t


def kernel(x_prompt, x_sample, cache_ckv, cache_kpe, state_ssm, state_conv, page_table, w_in, b_gate, q_norm_g, kv_norm_g, w_uq, w_ukv, conv_w, conv_b, dt_bias, a_log, d_skip, ssm_norm_g, w_branch_a, w_branch_b, w_out, ln1_g, ln1_b, w_ffn_gate, w_ffn_up, w_ffn_down, ln2_g, ln2_b):
    assert w_in.shape[0] == DEPTH == 1
    batch, seq, _ = x_prompt.shape
    dec_batch, t_new, _ = x_sample.shape
    n_pages = page_table.shape[1]
    past_len = n_pages * PAGE_SIZE
    n_p = batch * seq
    n_s = dec_batch * t_new

    w_in_p, wq_prompt, wq_sample, wkv_prompt, wuk_t, wuv = _prep_weights(w_in[0], w_uq[0], w_ukv[0])
    row2 = lambda v: v.reshape(1, -1).astype(F32)
    gq, gkv = row2(q_norm_g[0]), row2(kv_norm_g[0])
    pad_heads = lambda v: jnp.concatenate([v.astype(F32), jnp.zeros((LANES - SSM_HEADS,), F32)]).reshape(1, LANES)
    dtb, alog = pad_heads(dt_bias[0]), pad_heads(a_log[0])
    dskip = jnp.repeat(d_skip[0].astype(F32), SSM_HEADDIM).reshape(1, D_INNER)
    cw, cb = conv_w[0].astype(F32), row2(conv_b[0])
    bg, ng = row2(b_gate[0]), row2(ssm_norm_g[0])
    wa, wb, wo = w_branch_a[0].astype(BF16), w_branch_b[0].astype(BF16), w_out[0].astype(BF16)
    wfg, wfu, wfd = w_ffn_gate[0].astype(BF16), w_ffn_up[0].astype(BF16), w_ffn_down[0].astype(BF16)
    l1g, l1b, l2g, l2b = row2(ln1_g[0]), row2(ln1_b[0]), row2(ln2_g[0]), row2(ln2_b[0])

    xp = x_prompt.reshape(n_p, D_MODEL)
    proj_p = _matmul(xp, w_in_p, _tile_rows(n_p, 1024), 1024, F32)
    cos_p, sin_p, _, _ = _rope_tables(jnp.arange(seq))
    q_p, k_p, v_p, ckv_p, kpe_p = _mla_prep_prompt(proj_p, cos_p, sin_p, gq, gkv, wq_prompt, wkv_prompt,
                                                   _tile_rows(seq, 256))
    attn_p = _prompt_attention(q_p, k_p, v_p, batch, seq, _tile_rows(seq, 512))
    chunk_p = min(SSD_CHUNK, seq)
    y_p, ssm_p, conv_p = _ssd(proj_p.reshape(batch, seq, PROJ_COLS), None, None, cw, cb, dtb, alog, dskip,
                              chunk_p, chunk_p)
    h_p = _merge(xp, attn_p, y_p.reshape(n_p, D_INNER), proj_p, bg, ng, wa, wb, wo, l1g, l1b, _tile_rows(n_p, 256))
    out_p = _ffn(h_p, wfg, wfu, wfd, l2g, l2b, _tile_rows(n_p, 256))

    xs = x_sample.reshape(n_s, D_MODEL)
    proj_s = _matmul(xs, w_in_p, _tile_rows(n_s, 512), 1024, F32)
    cos_s, sin_s, cosp_s, sinp_s = _rope_tables(past_len + jnp.arange(t_new))
    tm_s = t_new * _tile_rows(dec_batch, 64)
    rep = lambda t: jnp.tile(t, (tm_s // t_new, 1))
    qlat_s, qpe_s, ckv_s, kpe_s = _mla_prep_sample(proj_s, rep(cos_s), rep(sin_s), rep(cosp_s), rep(sinp_s),
                                                   gq, gkv, wq_sample, wuk_t, tm_s)
    pps = _tile_rows(n_pages, 64)
    kpe_t = jnp.swapaxes(cache_kpe.reshape(cache_kpe.shape[1:]), 1, 2)
    olat_s = _paged_attention(page_table, qlat_s.reshape(n_s * MLA_HEADS, KV_LORA),
                              qpe_s.reshape(n_s * MLA_HEADS, QK_ROPE),
                              ckv_s.reshape(dec_batch, t_new, KV_LORA), kpe_s.reshape(dec_batch, t_new, QK_ROPE),
                              cache_ckv.reshape(cache_ckv.shape[1:]), kpe_t, t_new, pps)
    attn_s = _uv_proj(olat_s.reshape(n_s, MLA_HEADS * KV_LORA), wuv, _tile_rows(n_s, 256))
    conv0 = jnp.concatenate([jnp.zeros((dec_batch, SUBLANES - (CONV_WIDTH - 1), CONV_DIM), F32),
                             state_conv[0].astype(F32)], axis=1)
    ssm0 = state_ssm.astype(F32).reshape(dec_batch, SSM_HEADS * SSM_HEADDIM, SSM_STATE)
    y_s, ssm_s, conv_s = _ssd(proj_s.reshape(dec_batch, t_new, PROJ_COLS), conv0, ssm0, cw, cb, dtb, alog, dskip,
                              t_new, SUBLANES * pl.cdiv(t_new, SUBLANES))
    h_s = _merge(xs, attn_s, y_s.reshape(n_s, D_INNER), proj_s, bg, ng, wa, wb, wo, l1g, l1b, _tile_rows(n_s, 256))
    out_s = _ffn(h_s, wfg, wfu, wfd, l2g, l2b, _tile_rows(n_s, 256))

    st_shape = (SSM_HEADS, SSM_HEADDIM, SSM_STATE)
    return (out_p.reshape(batch, seq, D_MODEL), out_s.reshape(dec_batch, t_new, D_MODEL),
            ckv_p.reshape(1, batch, seq, KV_LORA), kpe_p.reshape(1, batch, seq, QK_ROPE),
            ssm_p.reshape(1, batch, *st_shape), conv_p.reshape(1, batch, CONV_WIDTH - 1, CONV_DIM),
            ckv_s.reshape(1, dec_batch, t_new, KV_LORA), kpe_s.reshape(1, dec_batch, t_new, QK_ROPE),
            ssm_s.reshape(1, dec_batch, *st_shape), conv_s.reshape(1, dec_batch, CONV_WIDTH - 1, CONV_DIM))
```

```python
import functools
import math

import jax
import jax.numpy as jnp
from jax import lax
from jax.experimental import pallas as pl
from jax.experimental.pallas import tpu as pltpu

F32 = jnp.float32
BF16 = jnp.bfloat16

D_MODEL = 1024
MLA_HEADS = 16
QK_NOPE = 64
QK_ROPE = 32
V_HEAD = 64
Q_LORA = 384
KV_LORA = 256
ROPE_THETA = 10000.0
ATTN_SCALE = (QK_NOPE + QK_ROPE) ** -0.5
Q_PRESCALE = ATTN_SCALE * math.log2(math.e)
D_INNER = 2 * D_MODEL
SSM_HEADDIM = 64
SSM_HEADS = D_INNER // SSM_HEADDIM
SSM_GROUPS = 4
HEADS_PER_GROUP = SSM_HEADS // SSM_GROUPS
SSM_STATE = 128
CONV_WIDTH = 4
CONV_DIM = D_INNER + 2 * SSM_GROUPS * SSM_STATE
SSD_CHUNK = 256
D_FF = ((8 * D_MODEL // 3 + 255) // 256) * 256
PAGE_SIZE = 128
DEPTH = 1
ALPHA = (2.0 * DEPTH) ** 0.25
IN_SPLITS = (Q_LORA, KV_LORA, QK_ROPE, D_INNER, CONV_DIM, SSM_HEADS, 2 * D_MODEL)
IN_OFFSETS = tuple(sum(IN_SPLITS[:i + 1]) for i in range(len(IN_SPLITS) - 1))

LANES = 128
SUBLANES = 8
VMEM_LIMIT_BYTES = 56 * 1024 * 1024

PROJ_XBC = 0
PROJ_SMALL = CONV_DIM
PROJ_Z = 4096
PROJ_G = 6144
PROJ_COLS = 8192
SMALL_CQ = 0
SMALL_CKV = Q_LORA
SMALL_KPE = Q_LORA + KV_LORA
SMALL_KROT = SMALL_KPE + LANES
SMALL_DT = SMALL_KROT + LANES
SMALL_COLS = 1024
PE_LANE0 = QK_NOPE


def _cparams(*sem):
    return pltpu.CompilerParams(dimension_semantics=sem, vmem_limit_bytes=VMEM_LIMIT_BYTES)


def _nt_dot(a, b):
    return lax.dot_general(a, b, (((1,), (1,)), ((), ())), preferred_element_type=F32)


def _silu(x):
    return x * (1.0 / (1.0 + jnp.exp(-x)))


def _sigmoid(x):
    return 1.0 / (1.0 + jnp.exp(-x))


def _softplus(x):
    return jnp.maximum(x, 0.0) + jnp.log(1.0 + jnp.exp(-jnp.abs(x)))


def _matmul_kernel(x_ref, wt_ref, o_ref):
    o_ref[...] = _nt_dot(x_ref[...].astype(BF16), wt_ref[...]).astype(o_ref.dtype)


def _matmul(x, wt, tm, tn, out_dtype):
    m, k = x.shape
    n = wt.shape[0]
    return pl.pallas_call(
        _matmul_kernel,
        grid=(m // tm, n // tn),
        in_specs=[pl.BlockSpec((tm, k), lambda i, j: (i, 0)),
                  pl.BlockSpec((tn, k), lambda i, j: (j, 0))],
        out_specs=pl.BlockSpec((tm, tn), lambda i, j: (i, j)),
        out_shape=jax.ShapeDtypeStruct((m, n), out_dtype),
        compiler_params=_cparams("parallel", "arbitrary"),
        name="in_proj",
    )(x, wt)


def _rms(x, g, eps):
    return x * lax.rsqrt(jnp.mean(x * x, axis=-1, keepdims=True) + eps) * g


def _mla_common(blk, cos, sin, gq_ref, gkv_ref, ckv_ref, kpe_ref):
    cqn = _rms(blk[:, SMALL_CQ:SMALL_CQ + Q_LORA], gq_ref[...], 1e-6).astype(BF16)
    ckvn = _rms(blk[:, SMALL_CKV:SMALL_CKV + KV_LORA], gkv_ref[...], 1e-6)
    ckv_ref[...] = ckvn
    kr = blk[:, SMALL_KPE:SMALL_KPE + LANES] * cos + blk[:, SMALL_KROT:SMALL_KROT + LANES] * sin
    kpe_ref[...] = kr[:, PE_LANE0:PE_LANE0 + QK_ROPE]
    return cqn, ckvn, kr


def _mla_prep_prompt_kernel(blk_ref, cos_ref, sin_ref, gq_ref, gkv_ref, wq_ref, wkv_ref,
                            q_ref, k_ref, vt_ref, ckv_ref, kpe_ref):
    cos = cos_ref[...]
    sin = sin_ref[...]
    cqn, ckvn, kr = _mla_common(blk_ref[...], cos, sin, gq_ref, gkv_ref, ckv_ref, kpe_ref)
    hw = MLA_HEADS * LANES
    qab = jnp.dot(cqn, wq_ref[...], preferred_element_type=F32)
    kv = jnp.dot(ckvn.astype(BF16), wkv_ref[...], preferred_element_type=F32)
    for h in range(MLA_HEADS):
        sl = slice(h * LANES, (h + 1) * LANES)
        q_ref[:, sl] = ((qab[:, sl] * cos + qab[:, hw + h * LANES:hw + (h + 1) * LANES] * sin)
                        * Q_PRESCALE).astype(BF16)
        k_ref[:, sl] = (kv[:, sl] + kr).astype(BF16)
    vt_ref[...] = kv[:, hw:].T.astype(BF16)


def _mla_prep_sample_kernel(blk_ref, cos_ref, sin_ref, cosp_ref, sinp_ref, gq_ref, gkv_ref,
                            wq_ref, wuk_ref, qlat_ref, qpe_ref, ckv_ref, kpe_ref):
    cqn, _, _ = _mla_common(blk_ref[...], cos_ref[...], sin_ref[...], gq_ref, gkv_ref, ckv_ref, kpe_ref)
    hw = MLA_HEADS * LANES
    pw = MLA_HEADS * QK_ROPE
    qall = jnp.dot(cqn, wq_ref[...], preferred_element_type=F32)
    qpe_ref[...] = ((qall[:, hw:hw + pw] * cosp_ref[...] + qall[:, hw + pw:] * sinp_ref[...])
                    * Q_PRESCALE).astype(BF16)
    for h in range(MLA_HEADS):
        qn = qall[:, h * LANES:(h + 1) * LANES].astype(BF16)
        qlat_ref[:, h * KV_LORA:(h + 1) * KV_LORA] = (jnp.dot(
            qn, wuk_ref[h], preferred_element_type=F32) * Q_PRESCALE).astype(BF16)


def _const_spec(shape):
    nd = len(shape)
    return pl.BlockSpec(shape, lambda *_: (0,) * nd)


def _mla_prep_prompt(proj, cos, sin, gq, gkv, wq, wkv, tm):
    n = proj.shape[0]
    hw = MLA_HEADS * LANES
    small_blk = PROJ_SMALL // SMALL_COLS
    row = lambda w: pl.BlockSpec((tm, w), lambda i: (i, 0))
    period = cos.shape[0] // tm
    tab = lambda w: pl.BlockSpec((tm, w), lambda i: (i % period, 0))
    return pl.pallas_call(
        _mla_prep_prompt_kernel,
        grid=(n // tm,),
        in_specs=[pl.BlockSpec((tm, SMALL_COLS), lambda i: (i, small_blk)),
                  tab(LANES), tab(LANES),
                  _const_spec(gq.shape), _const_spec(gkv.shape),
                  _const_spec(wq.shape), _const_spec(wkv.shape)],
        out_specs=[row(hw), row(hw),
                   pl.BlockSpec((None, MLA_HEADS * V_HEAD, tm), lambda i: (i // period, 0, i % period)),
                   row(KV_LORA), row(QK_ROPE)],
        out_shape=[jax.ShapeDtypeStruct((n, hw), BF16), jax.ShapeDtypeStruct((n, hw), BF16),
                   jax.ShapeDtypeStruct((n // cos.shape[0], MLA_HEADS * V_HEAD, cos.shape[0]), BF16),
                   jax.ShapeDtypeStruct((n, KV_LORA), F32), jax.ShapeDtypeStruct((n, QK_ROPE), F32)],
        compiler_params=_cparams("parallel"),
        name="mla_prep_prompt",
    )(proj, cos, sin, gq, gkv, wq, wkv)


def _mla_prep_sample(proj, cos, sin, cosp, sinp, gq, gkv, wq, wuk, tm):
    n = proj.shape[0]
    pw = MLA_HEADS * QK_ROPE
    small_blk = PROJ_SMALL // SMALL_COLS
    row = lambda w: pl.BlockSpec((tm, w), lambda i: (i, 0))
    tab = lambda w: pl.BlockSpec((tm, w), lambda i: (0, 0))
    return pl.pallas_call(
        _mla_prep_sample_kernel,
        grid=(n // tm,),
        in_specs=[pl.BlockSpec((tm, SMALL_COLS), lambda i: (i, small_blk)),
                  tab(LANES), tab(LANES), tab(pw), tab(pw),
                  _const_spec(gq.shape), _const_spec(gkv.shape),
                  _const_spec(wq.shape), _const_spec(wuk.shape)],
        out_specs=[row(MLA_HEADS * KV_LORA), row(pw), row(KV_LORA), row(QK_ROPE)],
        out_shape=[jax.ShapeDtypeStruct((n, MLA_HEADS * KV_LORA), BF16),
                   jax.ShapeDtypeStruct((n, pw), BF16),
                   jax.ShapeDtypeStruct((n, KV_LORA), F32), jax.ShapeDtypeStruct((n, QK_ROPE), F32)],
        compiler_params=_cparams("parallel"),
        name="mla_prep_sample",
    )(proj, cos, sin, cosp, sinp, gq, gkv, wq, wuk)


def _flash_kernel(q_ref, k_ref, vt_ref, o_ref, *, tq, tk):
    qi = pl.program_id(2)
    q2 = q_ref[...]
    qs = (q2[:, :LANES], q2[:, LANES:])
    neg = jnp.finfo(F32).min

    def scores(j):
        start = pl.multiple_of(j * tk, tk)
        return tuple(_nt_dot(k_ref[pl.ds(start, tk), hh * LANES:(hh + 1) * LANES], qs[hh])
                     for hh in range(2))

    def update(stats, sts, j):
        start = pl.multiple_of(j * tk, tk)
        new = []
        for hh in range(2):
            m, l, acc = stats[hh]
            vt = vt_ref[hh * V_HEAD:(hh + 1) * V_HEAD, pl.ds(start, tk)]
            m_new = jnp.maximum(m, jnp.max(sts[hh], axis=0, keepdims=True))
            alpha = jnp.exp2(m - m_new)
            p = jnp.exp2(sts[hh] - m_new)
            l = alpha * l + jnp.sum(p, axis=0, keepdims=True)
            acc = alpha * acc + jnp.dot(vt, p.astype(BF16), preferred_element_type=F32)
            new.append((m_new, l, acc))
        return tuple(new)

    def body(j, carry):
        stats, sts = carry
        nxt = scores(j + 1)
        return update(stats, sts, j), nxt

    one = (jnp.full((1, tq), -jnp.inf, F32), jnp.zeros((1, tq), F32), jnp.zeros((V_HEAD, tq), F32))
    n_diag = tq // tk
    n_full = qi * n_diag
    stats, sts = lax.fori_loop(0, n_full, body, ((one, one), scores(0)))
    key = lax.broadcasted_iota(jnp.int32, (tk, tq), 0)
    qry = lax.broadcasted_iota(jnp.int32, (tk, tq), 1)
    for d in range(n_diag):
        nxt = scores(n_full + d + 1) if d + 1 < n_diag else None
        visible = key + d * tk <= qry
        stats = update(stats, tuple(jnp.where(visible, st, neg) for st in sts), n_full + d)
        sts = nxt
    (_, l0, acc0), (_, l1, acc1) = stats
    o_ref[...] = jnp.concatenate([acc0 / l0, acc1 / l1], axis=0).T.astype(o_ref.dtype)


def _prompt_attention(q, k, vt, batch, seq, tq, tk):
    n = batch * seq
    nq = seq // tq
    pairs = MLA_HEADS // 2
    return pl.pallas_call(
        functools.partial(_flash_kernel, tq=tq, tk=tk),
        grid=(batch, pairs, nq),
        in_specs=[pl.BlockSpec((tq, 2 * LANES), lambda b, p, i: (b * nq + i, p)),
                  pl.BlockSpec((seq, 2 * LANES), lambda b, p, i: (b, p)),
                  pl.BlockSpec((None, 2 * V_HEAD, seq), lambda b, p, i: (b, p, 0))],
        out_specs=pl.BlockSpec((tq, 2 * V_HEAD), lambda b, p, i: (b * nq + i, p)),
        out_shape=jax.ShapeDtypeStruct((n, MLA_HEADS * V_HEAD), BF16),
        compiler_params=_cparams("parallel", "parallel", "arbitrary"),
        name="prompt_attention",
    )(q, k, vt)


def _paged_attn_kernel(pt_ref, qlat_ref, qpe_ref, ckvn_ref, kpen_ref, ckv_hbm, kpe_hbm, o_ref,
                       ckv_buf, kpe_buf, sem, m_sc, l_sc, acc_sc, *, pps, sub_pages, t_new, n_chunks):
    step = pl.program_id(0)
    n_steps = pl.num_programs(0)
    c = step % n_chunks
    slot = step % 2
    rows = t_new * MLA_HEADS

    def page_copies(s, sl):
        out = []
        for k in range(pps):
            page = pt_ref[s * pps + k]
            tok = pl.ds(k * PAGE_SIZE, PAGE_SIZE)
            out.append(pltpu.make_async_copy(ckv_hbm.at[page], ckv_buf.at[sl, tok, :], sem.at[sl]))
            out.append(pltpu.make_async_copy(kpe_hbm.at[page], kpe_buf.at[sl, :, tok], sem.at[sl]))
        return out

    @pl.when(step == 0)
    def _():
        for cp in page_copies(step, slot):
            cp.start()

    @pl.when(c == 0)
    def _():
        m_sc[...] = jnp.full(m_sc.shape, -jnp.inf, F32)
        l_sc[...] = jnp.zeros(l_sc.shape, F32)
        acc_sc[...] = jnp.zeros(acc_sc.shape, F32)

    for cp in page_copies(step, slot):
        cp.wait()

    nxt = jnp.where(step + 1 < n_steps, step + 1, 0)
    for cp in page_copies(nxt, 1 - slot):
        cp.start()

    qlat = qlat_ref[...]
    qpe = qpe_ref[...]
    cks, scores = [], []
    for j in range(pps // sub_pages):
        tok = pl.ds(j * sub_pages * PAGE_SIZE, sub_pages * PAGE_SIZE)
        ck = ckv_buf[slot, tok, :].astype(BF16)
        kp = kpe_buf[slot, :, tok].astype(BF16)
        cks.append(ck)
        scores.append(_nt_dot(qlat, ck) + jnp.dot(qpe, kp, preferred_element_type=F32))
    m_run, l_run, acc = m_sc[...], l_sc[...], acc_sc[...]
    for ck, s in zip(cks, scores):
        m_new = jnp.maximum(m_run, jnp.max(s, axis=-1, keepdims=True))
        alpha = jnp.exp2(m_run - m_new)
        p = jnp.exp2(s - m_new)
        l_run = alpha * l_run + jnp.sum(p, axis=-1, keepdims=True)
        acc = alpha * acc + jnp.dot(p.astype(BF16), ck, preferred_element_type=F32)
        m_run = m_new
    m_sc[...], l_sc[...], acc_sc[...] = m_run, l_run, acc

    @pl.when(c == n_chunks - 1)
    def _():
        qlf = qlat.astype(F32)
        qpf = qpe.astype(F32)
        cn = ckvn_ref[...].astype(BF16).astype(F32)
        kn = kpen_ref[...].astype(BF16).astype(F32)
        tok = lax.broadcasted_iota(jnp.int32, (rows, 1), 0) // MLA_HEADS
        neg = jnp.finfo(F32).min
        s_new = []
        for j in range(t_new):
            sj = (jnp.sum(qlf * cn[j:j + 1, :], axis=-1, keepdims=True)
                  + jnp.sum(qpf * kn[j:j + 1, :], axis=-1, keepdims=True))
            s_new.append(jnp.where(tok >= j, sj, neg))
        m1 = m_sc[...]
        m2 = m1
        for sj in s_new:
            m2 = jnp.maximum(m2, sj)
        a2 = jnp.exp2(m1 - m2)
        l2 = a2 * l_sc[...]
        acc2 = a2 * acc_sc[...]
        for j, sj in enumerate(s_new):
            pj = jnp.exp2(sj - m2)
            l2 = l2 + pj
            acc2 = acc2 + pj.astype(BF16).astype(F32) * cn[j:j + 1, :]
        o_ref[...] = (acc2 / l2).astype(o_ref.dtype)

    @pl.when(step == n_steps - 1)
    def _():
        for cp in page_copies(nxt, 1 - slot):
            cp.wait()


def _paged_attention(page_table, qlat, qpe, ckv_new, kpe_new, cache_ckv, cache_kpe, t_new, pps):
    dec_batch, n_pages = page_table.shape
    rows = t_new * MLA_HEADS
    n_chunks = n_pages // pps
    by_seq = lambda s, pt: (s // n_chunks, 0)
    by_seq3 = lambda s, pt: (s // n_chunks, 0, 0)
    grid_spec = pltpu.PrefetchScalarGridSpec(
        num_scalar_prefetch=1,
        grid=(dec_batch * n_chunks,),
        in_specs=[pl.BlockSpec((rows, KV_LORA), by_seq),
                  pl.BlockSpec((rows, QK_ROPE), by_seq),
                  pl.BlockSpec((None, t_new, KV_LORA), by_seq3),
                  pl.BlockSpec((None, t_new, QK_ROPE), by_seq3),
                  pl.BlockSpec(memory_space=pl.ANY),
                  pl.BlockSpec(memory_space=pl.ANY)],
        out_specs=pl.BlockSpec((rows, KV_LORA), by_seq),
        scratch_shapes=[pltpu.VMEM((2, pps * PAGE_SIZE, KV_LORA), F32),
                        pltpu.VMEM((2, QK_ROPE, pps * PAGE_SIZE), F32),
                        pltpu.SemaphoreType.DMA((2,)),
                        pltpu.VMEM((rows, 1), F32), pltpu.VMEM((rows, 1), F32),
                        pltpu.VMEM((rows, KV_LORA), F32)],
    )
    return pl.pallas_call(
        functools.partial(_paged_attn_kernel, pps=pps, sub_pages=_tile_rows(pps, 16), t_new=t_new,
                          n_chunks=n_chunks),
        grid_spec=grid_spec,
        out_shape=jax.ShapeDtypeStruct((dec_batch * rows, KV_LORA), BF16),
        compiler_params=_cparams("arbitrary"),
        name="paged_attention",
    )(page_table.reshape(-1), qlat, qpe, ckv_new, kpe_new, cache_ckv, cache_kpe)


def _uv_proj_kernel(olat_ref, wuv_ref, o_ref):
    for h in range(MLA_HEADS):
        o_ref[:, h * V_HEAD:(h + 1) * V_HEAD] = jnp.dot(
            olat_ref[:, h * KV_LORA:(h + 1) * KV_LORA], wuv_ref[h],
            preferred_element_type=F32).astype(o_ref.dtype)


def _uv_proj(olat, wuv, tm):
    n = olat.shape[0]
    return pl.pallas_call(
        _uv_proj_kernel,
        grid=(n // tm,),
        in_specs=[pl.BlockSpec((tm, MLA_HEADS * KV_LORA), lambda i: (i, 0)), _const_spec(wuv.shape)],
        out_specs=pl.BlockSpec((tm, MLA_HEADS * V_HEAD), lambda i: (i, 0)),
        out_shape=jax.ShapeDtypeStruct((n, MLA_HEADS * V_HEAD), BF16),
        compiler_params=_cparams("parallel"),
        name="uv_proj",
    )(olat, wuv)


def _split3(x):
    hi = x.astype(BF16)
    r = x - hi.astype(F32)
    mid = r.astype(BF16)
    lo = (r - mid.astype(F32)).astype(BF16)
    return hi, mid, lo


def _expand_heads(v, expand_ref):
    hi = v.astype(BF16)
    mid = (v - hi.astype(F32)).astype(BF16)
    return jnp.dot(jnp.concatenate([hi, mid], axis=1), expand_ref[...], preferred_element_type=F32)


def _ssd_heads_wide(xc, dt, a_cum, a_cum_t, exp_cum, dte, tril, dskip_ref, expand_ref, y_ref, state_t):
    chunk = xc.shape[0]
    gw = HEADS_PER_GROUP * SSM_HEADDIM
    b_off = D_INNER
    c_off = D_INNER + SSM_GROUPS * SSM_STATE
    xs = xc[:, :D_INNER]
    dt_e = _expand_heads(dt, expand_ref)
    exp_cum_e = _expand_heads(exp_cum, expand_ref)
    dte_e = _expand_heads(dte, expand_ref)
    xdt = xs * dt_e
    xdt_b = xdt.astype(BF16)
    xdte_b = (xdt * dte_e).astype(BF16)
    skip = xs * dskip_ref[...]
    lane = lax.broadcasted_iota(jnp.int32, (chunk, LANES), 1)
    for g in range(SSM_GROUPS):
        gcols = slice(g * gw, (g + 1) * gw)
        bg = xc[:, b_off + g * SSM_STATE:b_off + (g + 1) * SSM_STATE].astype(BF16)
        cg = xc[:, c_off + g * SSM_STATE:c_off + (g + 1) * SSM_STATE].astype(BF16)
        cb = _nt_dot(cg, bg)
        st_g = state_t[:, gcols]
        y_off = jnp.dot(cg, st_g.astype(BF16), preferred_element_type=F32)
        for pair in range(HEADS_PER_GROUP // 2):
            h0 = g * HEADS_PER_GROUP + 2 * pair
            cols = slice(h0 * SSM_HEADDIM, (h0 + 2) * SSM_HEADDIM)
            rhs = xdt_b[:, cols]
            ys = []
            for h in (h0, h0 + 1):
                seg = a_cum[:, h:h + 1] - a_cum_t[h:h + 1, :]
                decay = jnp.exp(jnp.where(tril, seg, -jnp.inf))
                ys.append(jnp.dot((cb * decay).astype(BF16), rhs, preferred_element_type=F32))
            y = jnp.where(lane < SSM_HEADDIM, ys[0], ys[1])
            y_ref[:, cols] = (y + y_off[:, pair * LANES:(pair + 1) * LANES] * exp_cum_e[:, cols]
                              + skip[:, cols])
        cs_t = lax.dot_general(bg, xdte_b[:, gcols], (((0,), (0,)), ((), ())),
                               preferred_element_type=F32)
        state_t[:, gcols] = st_g * exp_cum_e[chunk - 1:chunk, gcols] + cs_t


def _ssd_kernel(*refs, l_in, chunk, has_init):
    if has_init:
        (xbc_ref, dt_ref, conv0_ref, ssm0_ref, cw_ref, cb_ref, dtb_ref, alog_ref, dskip_ref,
         y_ref, ssm_ref, conv_ref, xpad, state, dtpad) = refs
    else:
        (xbc_ref, dt_ref, cw_ref, cb_ref, dtb_ref, alog_ref, dskip_ref, expand_ref,
         y_ref, ssm_ref, conv_ref, xpad, state, dtpad) = refs
    c = pl.program_id(1)
    last = pl.num_programs(1) - 1
    tail = SUBLANES

    @pl.when(c == 0)
    def _():
        if has_init:
            xpad[0:tail, :] = conv0_ref[...]
            state[...] = ssm0_ref[...]
        else:
            xpad[0:tail, :] = jnp.zeros((tail, CONV_DIM), F32)
            state[...] = jnp.zeros(state.shape, F32)
        if l_in < chunk:
            xpad[tail:, :] = jnp.zeros((chunk, CONV_DIM), F32)

    xpad[tail:tail + l_in, :] = xbc_ref[...]

    conv = cb_ref[...] + cw_ref[CONV_WIDTH - 1:CONV_WIDTH, :] * xpad[tail:tail + chunk, :]
    for k in range(CONV_WIDTH - 1):
        off = tail - (CONV_WIDTH - 1) + k
        conv = conv + cw_ref[k:k + 1, :] * xpad[off:off + chunk, :]
    xc = _silu(conv)

    win = tail + ((l_in - (CONV_WIDTH - 1)) // SUBLANES) * SUBLANES
    sub = (l_in - (CONV_WIDTH - 1)) % SUBLANES

    @pl.when(c == last)
    def _():
        conv_ref[...] = xpad[win:win + SUBLANES, :][sub:sub + CONV_WIDTH - 1, :]

    xpad[0:tail, :] = xpad[chunk:chunk + tail, :]

    rows = lax.broadcasted_iota(jnp.int32, (chunk, LANES), 0)
    if l_in < chunk:
        dtpad[...] = jnp.zeros((chunk, LANES), F32)
        dtpad[0:l_in, :] = dt_ref[...]
        dt_raw = dtpad[...]
    else:
        dt_raw = dt_ref[...]
    dt = jnp.where(rows < l_in, _softplus(dt_raw + dtb_ref[...]), 0.0)
    a = -jnp.exp(alog_ref[...])
    da = dt * a

    ti = lax.broadcasted_iota(jnp.int32, (chunk, chunk), 0)
    si = lax.broadcasted_iota(jnp.int32, (chunk, chunk), 1)
    tril = ti >= si
    tril_b = tril.astype(BF16)
    hi, mid, lo = _split3(da)
    cum3 = jnp.dot(tril_b, jnp.concatenate([hi, mid, lo], axis=1), preferred_element_type=F32)
    a_cum = (cum3[:, 0:LANES] + cum3[:, LANES:2 * LANES]) + cum3[:, 2 * LANES:3 * LANES]
    a_cum_t = a_cum.T
    exp_cum = jnp.exp(a_cum)
    a_last = a_cum[chunk - 1:chunk, :]
    dte = jnp.exp(a_last - a_cum)

    gw = HEADS_PER_GROUP * SSM_HEADDIM
    b_off = D_INNER
    c_off = D_INNER + SSM_GROUPS * SSM_STATE

    if not has_init:
        assert l_in == chunk
        _ssd_heads_wide(xc, dt, a_cum, a_cum_t, exp_cum, dte, tril, dskip_ref, expand_ref, y_ref, state)

        @pl.when(c == last)
        def _():
            ssm_ref[...] = state[...].T
        return

    chunk_decay_t = jnp.exp(a_cum_t[:, chunk - 1:chunk])
    for g in range(SSM_GROUPS):
        bg = xc[:, b_off + g * SSM_STATE:b_off + (g + 1) * SSM_STATE].astype(BF16)
        cg = xc[:, c_off + g * SSM_STATE:c_off + (g + 1) * SSM_STATE].astype(BF16)
        cb = _nt_dot(cg, bg)
        st_g = state[g * gw:(g + 1) * gw, :]
        y_off = _nt_dot(cg, st_g.astype(BF16))
        xdte_parts = []
        for hl in range(HEADS_PER_GROUP):
            h = g * HEADS_PER_GROUP + hl
            xs = xc[:, h * SSM_HEADDIM:(h + 1) * SSM_HEADDIM]
            xdt = xs * dt[:, h:h + 1]
            seg = a_cum[:, h:h + 1] - a_cum_t[h:h + 1, :]
            decay = jnp.exp(jnp.where(tril, seg, -jnp.inf))
            y = jnp.dot((cb * decay).astype(BF16), xdt.astype(BF16), preferred_element_type=F32)
            y = y + y_off[:, hl * SSM_HEADDIM:(hl + 1) * SSM_HEADDIM] * exp_cum[:, h:h + 1]
            y = y + xs * dskip_ref[:, h * SSM_HEADDIM:(h + 1) * SSM_HEADDIM]
            y_ref[:, h * SSM_HEADDIM:(h + 1) * SSM_HEADDIM] = y[0:l_in, :]
            xdte_parts.append(xdt * dte[:, h:h + 1])
        xdte_t = jnp.concatenate(xdte_parts, axis=1).T.astype(BF16)
        cs = jnp.dot(xdte_t, bg, preferred_element_type=F32)
        for hl in range(HEADS_PER_GROUP):
            h = g * HEADS_PER_GROUP + hl
            r0 = g * gw + hl * SSM_HEADDIM
            state[r0:r0 + SSM_HEADDIM, :] = (st_g[hl * SSM_HEADDIM:(hl + 1) * SSM_HEADDIM, :]
                                             * chunk_decay_t[h:h + 1, :]
                                             + cs[hl * SSM_HEADDIM:(hl + 1) * SSM_HEADDIM, :])

    @pl.when(c == last)
    def _():
        ssm_ref[...] = state[...]


def _ssd(proj3, conv0, ssm0, cw, cb, dtb, alog, dskip, l_in, chunk):
    batch, seq, _ = proj3.shape
    nc = seq // l_in
    assert l_in == chunk or nc == 1
    has_init = conv0 is not None
    hp = SSM_HEADS * SSM_HEADDIM
    dt_blk = (PROJ_SMALL + SMALL_DT) // LANES
    in_specs = [pl.BlockSpec((None, l_in, CONV_DIM), lambda b, c: (b, c, PROJ_XBC // CONV_DIM)),
                pl.BlockSpec((None, l_in, LANES), lambda b, c: (b, c, dt_blk))]
    args = [proj3, proj3]
    if has_init:
        in_specs += [pl.BlockSpec((None, SUBLANES, CONV_DIM), lambda b, c: (b, 0, 0)),
                     pl.BlockSpec((None, hp, SSM_STATE), lambda b, c: (b, 0, 0))]
        args += [conv0, ssm0]
    consts = [cw, cb, dtb, alog, dskip]
    if not has_init:
        sel = (jnp.arange(LANES)[:, None] == jnp.arange(hp)[None, :] // SSM_HEADDIM).astype(BF16)
        consts.append(jnp.concatenate([sel, sel], axis=0))
    state_shape = (hp, SSM_STATE) if has_init else (SSM_STATE, hp)
    in_specs += [_const_spec(x.shape) for x in consts]
    args += consts
    return pl.pallas_call(
        functools.partial(_ssd_kernel, l_in=l_in, chunk=chunk, has_init=has_init),
        grid=(batch, nc),
        in_specs=in_specs,
        out_specs=[pl.BlockSpec((None, l_in, D_INNER), lambda b, c: (b, c, 0)),
                   pl.BlockSpec((None, hp, SSM_STATE), lambda b, c: (b, 0, 0)),
                   pl.BlockSpec((None, CONV_WIDTH - 1, CONV_DIM), lambda b, c: (b, 0, 0))],
        out_shape=[jax.ShapeDtypeStruct((batch, seq, D_INNER), F32),
                   jax.ShapeDtypeStruct((batch, hp, SSM_STATE), F32),
                   jax.ShapeDtypeStruct((batch, CONV_WIDTH - 1, CONV_DIM), F32)],
        scratch_shapes=[pltpu.VMEM((SUBLANES + chunk, CONV_DIM), F32), pltpu.VMEM(state_shape, F32),
                        pltpu.VMEM((chunk, LANES), F32)],
        compiler_params=_cparams("parallel", "arbitrary"),
        name="conv_ssd",
    )(*args)


def _layer_norm(x, g, b):
    mu = jnp.mean(x, axis=-1, keepdims=True)
    xc = x - mu
    var = jnp.mean(xc * xc, axis=-1, keepdims=True)
    return xc * lax.rsqrt(var + 1e-5) * g + b


def _merge_kernel(x_ref, attn_ref, y_ref, z_ref, g_ref, bg_ref, ng_ref, wa_ref, wb_ref, wo_ref,
                  l1g_ref, l1b_ref, h_ref):
    gsz = D_INNER // SSM_GROUPS
    yg = y_ref[...] * _silu(z_ref[...])
    parts = []
    for g in range(SSM_GROUPS):
        parts.append(_rms(yg[:, g * gsz:(g + 1) * gsz], ng_ref[:, g * gsz:(g + 1) * gsz], 1e-5).astype(BF16))
    ygn = jnp.concatenate(parts, axis=1)
    br_a = jnp.dot(attn_ref[...], wa_ref[...], preferred_element_type=F32)
    br_b = jnp.dot(ygn, wb_ref[...], preferred_element_type=F32)
    gates = _sigmoid(g_ref[...] + bg_ref[...])
    merged = gates[:, :D_MODEL] * br_a + gates[:, D_MODEL:] * br_b
    mix = jnp.dot(merged.astype(BF16), wo_ref[...], preferred_element_type=F32)
    h_ref[...] = _layer_norm(ALPHA * x_ref[...] + mix, l1g_ref[...], l1b_ref[...])


def _merge(x, attn, y, proj, bg, ng, wa, wb, wo, l1g, l1b, tm):
    n = x.shape[0]
    row = lambda w: pl.BlockSpec((tm, w), lambda i: (i, 0))
    consts = [bg, ng, wa, wb, wo, l1g, l1b]
    return pl.pallas_call(
        _merge_kernel,
        grid=(n // tm,),
        in_specs=[row(D_MODEL), row(MLA_HEADS * V_HEAD), row(D_INNER),
                  pl.BlockSpec((tm, D_INNER), lambda i: (i, PROJ_Z // D_INNER)),
                  pl.BlockSpec((tm, 2 * D_MODEL), lambda i: (i, PROJ_G // (2 * D_MODEL)))]
                 + [_const_spec(c.shape) for c in consts],
        out_specs=row(D_MODEL),
        out_shape=jax.ShapeDtypeStruct((n, D_MODEL), F32),
        compiler_params=_cparams("parallel"),
        name="merge_ln1",
    )(x, attn, y, proj, proj, *consts)


def _ffn_kernel(h_ref, wg_ref, wu_ref, wd_ref, l2g_ref, l2b_ref, o_ref):
    h = h_ref[...]
    hb = h.astype(BF16)
    g = jnp.dot(hb, wg_ref[...], preferred_element_type=F32)
    u = jnp.dot(hb, wu_ref[...], preferred_element_type=F32)
    f = jnp.dot((_silu(g) * u).astype(BF16), wd_ref[...], preferred_element_type=F32)
    o_ref[...] = _layer_norm(ALPHA * h + f, l2g_ref[...], l2b_ref[...])


def _ffn(h, wg, wu, wd, l2g, l2b, tm):
    n = h.shape[0]
    row = pl.BlockSpec((tm, D_MODEL), lambda i: (i, 0))
    consts = [wg, wu, wd, l2g, l2b]
    return pl.pallas_call(
        _ffn_kernel,
        grid=(n // tm,),
        in_specs=[row] + [_const_spec(c.shape) for c in consts],
        out_specs=row,
        out_shape=jax.ShapeDtypeStruct((n, D_MODEL), F32),
        compiler_params=_cparams("parallel"),
        name="ffn_ln2",
    )(h, *consts)


def _rot_half(w):
    half = QK_ROPE // 2
    return jnp.concatenate([-w[..., half:], w[..., :half]], axis=-1)


def _prep_weights(w_in, w_uq, w_ukv):
    wq, wkv, wkpe, wz, wxbc, wdt, wg = jnp.split(w_in.T.astype(BF16), IN_OFFSETS, axis=0)
    zr = lambda n: jnp.zeros((n, D_MODEL), BF16)
    pad_hi = LANES - PE_LANE0 - QK_ROPE
    half = QK_ROPE // 2
    krot = jnp.concatenate([-wkpe[half:], wkpe[:half]], axis=0)
    small = jnp.concatenate([wq, wkv, zr(PE_LANE0), wkpe, zr(pad_hi), zr(PE_LANE0), krot, zr(pad_hi),
                             wdt, zr(LANES - SSM_HEADS)], axis=0)
    w_in_p = jnp.concatenate([wxbc, small, wz, wg], axis=0)

    uq = w_uq.reshape(Q_LORA, MLA_HEADS, QK_NOPE + QK_ROPE)
    nope, pe = uq[..., :QK_NOPE], uq[..., QK_NOPE:]
    rot = _rot_half(pe)
    zq = lambda n: jnp.zeros((Q_LORA, MLA_HEADS, n), F32)
    hw = MLA_HEADS * LANES
    wq_a = jnp.concatenate([nope, pe, zq(pad_hi)], axis=-1).reshape(Q_LORA, hw)
    wq_b = jnp.concatenate([zq(PE_LANE0), rot, zq(pad_hi)], axis=-1).reshape(Q_LORA, hw)
    wq_prompt = jnp.concatenate([wq_a, wq_b], axis=1).astype(BF16)
    wq_n = jnp.concatenate([nope, zq(LANES - QK_NOPE)], axis=-1).reshape(Q_LORA, hw)
    wq_sample = jnp.concatenate([wq_n, pe.reshape(Q_LORA, -1), rot.reshape(Q_LORA, -1)], axis=1).astype(BF16)

    ukv = w_ukv.reshape(KV_LORA, MLA_HEADS, QK_NOPE + V_HEAD)
    k_nope, v = ukv[..., :QK_NOPE], ukv[..., QK_NOPE:]
    zk = jnp.zeros((KV_LORA, MLA_HEADS, LANES - QK_NOPE), F32)
    wkv_prompt = jnp.concatenate([jnp.concatenate([k_nope, zk], axis=-1).reshape(KV_LORA, hw),
                                  v.reshape(KV_LORA, -1)], axis=1).astype(BF16)
    wuk_t = jnp.concatenate([k_nope, zk], axis=-1).transpose(1, 2, 0).astype(BF16)
    wuv = v.transpose(1, 0, 2).astype(BF16)
    return w_in_p, wq_prompt, wq_sample, wkv_prompt, wuk_t, wuv


def _rope_tables(pos):
    half = QK_ROPE // 2
    inv = ROPE_THETA ** (-jnp.arange(0, QK_ROPE, 2, dtype=F32) / QK_ROPE)
    ang = pos.astype(F32)[:, None] * inv[None, :]
    c2 = jnp.concatenate([jnp.cos(ang), jnp.cos(ang)], axis=1)
    s2 = jnp.concatenate([jnp.sin(ang), jnp.sin(ang)], axis=1)
    t = pos.shape[0]
    cos = jnp.concatenate([jnp.ones((t, PE_LANE0), F32), c2, jnp.zeros((t, LANES - PE_LANE0 - QK_ROPE), F32)], axis=1)
    sin = jnp.concatenate([jnp.zeros((t, PE_LANE0), F32), s2, jnp.zeros((t, LANES - PE_LANE0 - QK_ROPE), F32)], axis=1)
    return cos, sin, jnp.tile(c2, (1, MLA_HEADS)), jnp.tile(s2, (1, MLA_HEADS))


def _tile_rows(n, pref):
    t = min(pref, n)
    while n % t:
        t //= 2
    return t


def kernel(x_prompt, x_sample, cache_ckv, cache_kpe, state_ssm, state_conv, page_table, w_in, b_gate, q_norm_g, kv_norm_g, w_uq, w_ukv, conv_w, conv_b, dt_bias, a_log, d_skip, ssm_norm_g, w_branch_a, w_branch_b, w_out, ln1_g, ln1_b, w_ffn_gate, w_ffn_up, w_ffn_down, ln2_g, ln2_b):
    assert w_in.shape[0] == DEPTH == 1
    batch, seq, _ = x_prompt.shape
    dec_batch, t_new, _ = x_sample.shape
    n_pages = page_table.shape[1]
    past_len = n_pages * PAGE_SIZE
    n_p = batch * seq
    n_s = dec_batch * t_new

    w_in_p, wq_prompt, wq_sample, wkv_prompt, wuk_t, wuv = _prep_weights(w_in[0], w_uq[0], w_ukv[0])
    row2 = lambda v: v.reshape(1, -1).astype(F32)
    gq, gkv = row2(q_norm_g[0]), row2(kv_norm_g[0])
    pad_heads = lambda v: jnp.concatenate([v.astype(F32), jnp.zeros((LANES - SSM_HEADS,), F32)]).reshape(1, LANES)
    dtb, alog = pad_heads(dt_bias[0]), pad_heads(a_log[0])
    dskip = jnp.repeat(d_skip[0].astype(F32), SSM_HEADDIM).reshape(1, D_INNER)
    cw, cb = conv_w[0].astype(F32), row2(conv_b[0])
    bg, ng = row2(b_gate[0]), row2(ssm_norm_g[0])
    wa, wb, wo = w_branch_a[0].astype(BF16), w_branch_b[0].astype(BF16), w_out[0].astype(BF16)
    wfg, wfu, wfd = w_ffn_gate[0].astype(BF16), w_ffn_up[0].astype(BF16), w_ffn_down[0].astype(BF16)
    l1g, l1b, l2g, l2b = row2(ln1_g[0]), row2(ln1_b[0]), row2(ln2_g[0]), row2(ln2_b[0])

    xp = x_prompt.reshape(n_p, D_MODEL)
    proj_p = _matmul(xp, w_in_p, _tile_rows(n_p, 1024), 1024, F32)
    cos_p, sin_p, _, _ = _rope_tables(jnp.arange(seq))
    q_p, k_p, vt_p, ckv_p, kpe_p = _mla_prep_prompt(proj_p, cos_p, sin_p, gq, gkv, wq_prompt, wkv_prompt,
                                                   _tile_rows(seq, 256))
    tq_p = _tile_rows(seq, 512)
    attn_p = _prompt_attention(q_p, k_p, vt_p, batch, seq, tq_p, tq_p)
    chunk_p = min(SSD_CHUNK, seq)
    y_p, ssm_p, conv_p = _ssd(proj_p.reshape(batch, seq, PROJ_COLS), None, None, cw, cb, dtb, alog, dskip,
                              chunk_p, chunk_p)
    h_p = _merge(xp, attn_p, y_p.reshape(n_p, D_INNER), proj_p, bg, ng, wa, wb, wo, l1g, l1b, _tile_rows(n_p, 256))
    out_p = _ffn(h_p, wfg, wfu, wfd, l2g, l2b, _tile_rows(n_p, 256))

    xs = x_sample.reshape(n_s, D_MODEL)
    proj_s = _matmul(xs, w_in_p, _tile_rows(n_s, 512), 1024, F32)
    cos_s, sin_s, cosp_s, sinp_s = _rope_tables(past_len + jnp.arange(t_new))
    tm_s = t_new * _tile_rows(dec_batch, 64)
    rep = lambda t: jnp.tile(t, (tm_s // t_new, 1))
    qlat_s, qpe_s, ckv_s, kpe_s = _mla_prep_sample(proj_s, rep(cos_s), rep(sin_s), rep(cosp_s), rep(sinp_s),
                                                   gq, gkv, wq_sample, wuk_t, tm_s)
    pps = _tile_rows(n_pages, 64)
    kpe_t = jnp.swapaxes(cache_kpe.reshape(cache_kpe.shape[1:]), 1, 2)
    olat_s = _paged_attention(page_table, qlat_s.reshape(n_s * MLA_HEADS, KV_LORA),
                              qpe_s.reshape(n_s * MLA_HEADS, QK_ROPE),
                              ckv_s.reshape(dec_batch, t_new, KV_LORA), kpe_s.reshape(dec_batch, t_new, QK_ROPE),
                              cache_ckv.reshape(cache_ckv.shape[1:]), kpe_t, t_new, pps)
    attn_s = _uv_proj(olat_s.reshape(n_s, MLA_HEADS * KV_LORA), wuv, _tile_rows(n_s, 256))
    conv0 = jnp.concatenate([jnp.zeros((dec_batch, SUBLANES - (CONV_WIDTH - 1), CONV_DIM), F32),
                             state_conv[0].astype(F32)], axis=1)
    ssm0 = state_ssm.astype(F32).reshape(dec_batch, SSM_HEADS * SSM_HEADDIM, SSM_STATE)
    y_s, ssm_s, conv_s = _ssd(proj_s.reshape(dec_batch, t_new, PROJ_COLS), conv0, ssm0, cw, cb, dtb, alog, dskip,
                              t_new, SUBLANES * pl.cdiv(t_new, SUBLANES))
    h_s = _merge(xs, attn_s, y_s.reshape(n_s, D_INNER), proj_s, bg, ng, wa, wb, wo, l1g, l1b, _tile_rows(n_s, 256))
    out_s = _ffn(h_s, wfg, wfu, wfd, l2g, l2b, _tile_rows(n_s, 256))

    st_shape = (SSM_HEADS, SSM_HEADDIM, SSM_STATE)
    return (out_p.reshape(batch, seq, D_MODEL), out_s.reshape(dec_batch, t_new, D_MODEL),
            ckv_p.reshape(1, batch, seq, KV_LORA), kpe_p.reshape(1, batch, seq, QK_ROPE),
            ssm_p.reshape(1, batch, *st_shape), conv_p.reshape(1, batch, CONV_WIDTH - 1, CONV_DIM),
            ckv_s.reshape(1, dec_batch, t_new, KV_LORA), kpe_s.reshape(1, dec_batch, t_new, QK_ROPE),
            ssm_s.reshape(1, dec_batch, *st_shape), conv_s.reshape(1, dec_batch, CONV_WIDTH - 1, CONV_DIM))
```

```python
import functools
import math

import jax
import jax.numpy as jnp
from jax import lax
from jax.experimental import pallas as pl
from jax.experimental.pallas import tpu as pltpu

F32 = jnp.float32
BF16 = jnp.bfloat16

D_MODEL = 1024
MLA_HEADS = 16
QK_NOPE = 64
QK_ROPE = 32
V_HEAD = 64
Q_LORA = 384
KV_LORA = 256
ROPE_THETA = 10000.0
ATTN_SCALE = (QK_NOPE + QK_ROPE) ** -0.5
Q_PRESCALE = ATTN_SCALE * math.log2(math.e)
D_INNER = 2 * D_MODEL
SSM_HEADDIM = 64
SSM_HEADS = D_INNER // SSM_HEADDIM
SSM_GROUPS = 4
HEADS_PER_GROUP = SSM_HEADS // SSM_GROUPS
SSM_STATE = 128
CONV_WIDTH = 4
CONV_DIM = D_INNER + 2 * SSM_GROUPS * SSM_STATE
SSD_CHUNK = 256
D_FF = ((8 * D_MODEL // 3 + 255) // 256) * 256
PAGE_SIZE = 128
DEPTH = 1
ALPHA = (2.0 * DEPTH) ** 0.25
IN_SPLITS = (Q_LORA, KV_LORA, QK_ROPE, D_INNER, CONV_DIM, SSM_HEADS, 2 * D_MODEL)
IN_OFFSETS = tuple(sum(IN_SPLITS[:i + 1]) for i in range(len(IN_SPLITS) - 1))

LANES = 128
SUBLANES = 8
VMEM_LIMIT_BYTES = 56 * 1024 * 1024
PAGE_RING_SLOTS = 3

IN_PROJ_ROWS = 2048
IN_PROJ_COL_TILE = 1024
SAMPLE_PROJ_ROWS = 512
MLA_PREP_ROWS = 512
SAMPLE_PREP_SEQS = 64
ATTN_TILE = 512
DENSE_ROWS = 256
PAGES_PER_STEP = 64
SOFTMAX_SUB_PAGES = 16

PROJ_XBC = 0
PROJ_SMALL = CONV_DIM
PROJ_Z = 4096
PROJ_G = 6144
PROJ_COLS = 8192
SMALL_CQ = 0
SMALL_CKV = Q_LORA
SMALL_KPE = Q_LORA + KV_LORA
SMALL_KROT = SMALL_KPE + LANES
SMALL_DT = SMALL_KROT + LANES
SMALL_COLS = 1024
PE_LANE0 = QK_NOPE


def _cparams(*sem):
    return pltpu.CompilerParams(dimension_semantics=sem, vmem_limit_bytes=VMEM_LIMIT_BYTES)


def _nt_dot(a, b):
    return lax.dot_general(a, b, (((1,), (1,)), ((), ())), preferred_element_type=F32)


def _silu(x):
    return x * (1.0 / (1.0 + jnp.exp(-x)))


def _sigmoid(x):
    return 1.0 / (1.0 + jnp.exp(-x))


def _softplus(x):
    return jnp.maximum(x, 0.0) + jnp.log(1.0 + jnp.exp(-jnp.abs(x)))


def _matmul_kernel(x_ref, wt_ref, o_ref):
    o_ref[...] = _nt_dot(x_ref[...].astype(BF16), wt_ref[...]).astype(o_ref.dtype)


def _matmul(x, wt, tm, tn, out_dtype):
    m, k = x.shape
    n = wt.shape[0]
    return pl.pallas_call(
        _matmul_kernel,
        grid=(m // tm, n // tn),
        in_specs=[pl.BlockSpec((tm, k), lambda i, j: (i, 0)),
                  pl.BlockSpec((tn, k), lambda i, j: (j, 0))],
        out_specs=pl.BlockSpec((tm, tn), lambda i, j: (i, j)),
        out_shape=jax.ShapeDtypeStruct((m, n), out_dtype),
        compiler_params=_cparams("parallel", "arbitrary"),
        name="in_proj",
    )(x, wt)


def _rms(x, g, eps):
    return x * lax.rsqrt(jnp.mean(x * x, axis=-1, keepdims=True) + eps) * g


def _mla_common(blk, cos, sin, gq_ref, gkv_ref, ckv_ref, kpe_ref):
    cqn = _rms(blk[:, SMALL_CQ:SMALL_CQ + Q_LORA], gq_ref[...], 1e-6).astype(BF16)
    ckvn = _rms(blk[:, SMALL_CKV:SMALL_CKV + KV_LORA], gkv_ref[...], 1e-6)
    ckv_ref[...] = ckvn
    kr = blk[:, SMALL_KPE:SMALL_KPE + LANES] * cos + blk[:, SMALL_KROT:SMALL_KROT + LANES] * sin
    kpe_ref[...] = kr[:, PE_LANE0:PE_LANE0 + QK_ROPE]
    return cqn, ckvn, kr


def _mla_prep_prompt_kernel(blk_ref, cos_ref, sin_ref, gq_ref, gkv_ref, wq_ref, wkv_ref,
                            q_ref, k_ref, vt_ref, ckv_ref, kpe_ref):
    cos = cos_ref[...]
    sin = sin_ref[...]
    cqn, ckvn, kr = _mla_common(blk_ref[...], cos, sin, gq_ref, gkv_ref, ckv_ref, kpe_ref)
    hw = MLA_HEADS * LANES
    qab = jnp.dot(cqn, wq_ref[...], preferred_element_type=F32)
    kv = jnp.dot(ckvn.astype(BF16), wkv_ref[...], preferred_element_type=F32)
    for h in range(MLA_HEADS):
        sl = slice(h * LANES, (h + 1) * LANES)
        q_ref[:, sl] = ((qab[:, sl] * cos + qab[:, hw + h * LANES:hw + (h + 1) * LANES] * sin)
                        * Q_PRESCALE).astype(BF16)
        k_ref[:, sl] = (kv[:, sl] + kr).astype(BF16)
    vt_ref[...] = kv[:, hw:].T.astype(BF16)


def _mla_prep_sample_kernel(blk_ref, cos_ref, sin_ref, cosp_ref, sinp_ref, gq_ref, gkv_ref,
                            wq_ref, wuk_ref, qlat_ref, qpe_ref, ckv_ref, kpe_ref):
    cqn, _, _ = _mla_common(blk_ref[...], cos_ref[...], sin_ref[...], gq_ref, gkv_ref, ckv_ref, kpe_ref)
    hw = MLA_HEADS * LANES
    pw = MLA_HEADS * QK_ROPE
    qall = jnp.dot(cqn, wq_ref[...], preferred_element_type=F32)
    qpe_ref[...] = ((qall[:, hw:hw + pw] * cosp_ref[...] + qall[:, hw + pw:] * sinp_ref[...])
                    * Q_PRESCALE).astype(BF16)
    for h in range(MLA_HEADS):
        qn = qall[:, h * LANES:(h + 1) * LANES].astype(BF16)
        qlat_ref[:, h * KV_LORA:(h + 1) * KV_LORA] = (jnp.dot(
            qn, wuk_ref[h], preferred_element_type=F32) * Q_PRESCALE).astype(BF16)


def _const_spec(shape):
    nd = len(shape)
    return pl.BlockSpec(shape, lambda *_: (0,) * nd)


def _mla_prep_prompt(proj, cos, sin, gq, gkv, wq, wkv, tm):
    n = proj.shape[0]
    hw = MLA_HEADS * LANES
    small_blk = PROJ_SMALL // SMALL_COLS
    row = lambda w: pl.BlockSpec((tm, w), lambda i: (i, 0))
    period = cos.shape[0] // tm
    tab = lambda w: pl.BlockSpec((tm, w), lambda i: (i % period, 0))
    return pl.pallas_call(
        _mla_prep_prompt_kernel,
        grid=(n // tm,),
        in_specs=[pl.BlockSpec((tm, SMALL_COLS), lambda i: (i, small_blk)),
                  tab(LANES), tab(LANES),
                  _const_spec(gq.shape), _const_spec(gkv.shape),
                  _const_spec(wq.shape), _const_spec(wkv.shape)],
        out_specs=[row(hw), row(hw),
                   pl.BlockSpec((None, MLA_HEADS * V_HEAD, tm), lambda i: (i // period, 0, i % period)),
                   row(KV_LORA), row(QK_ROPE)],
        out_shape=[jax.ShapeDtypeStruct((n, hw), BF16), jax.ShapeDtypeStruct((n, hw), BF16),
                   jax.ShapeDtypeStruct((n // cos.shape[0], MLA_HEADS * V_HEAD, cos.shape[0]), BF16),
                   jax.ShapeDtypeStruct((n, KV_LORA), F32), jax.ShapeDtypeStruct((n, QK_ROPE), F32)],
        compiler_params=_cparams("parallel"),
        name="mla_prep_prompt",
    )(proj, cos, sin, gq, gkv, wq, wkv)


def _mla_prep_sample(proj, cos, sin, cosp, sinp, gq, gkv, wq, wuk, tm):
    n = proj.shape[0]
    pw = MLA_HEADS * QK_ROPE
    small_blk = PROJ_SMALL // SMALL_COLS
    row = lambda w: pl.BlockSpec((tm, w), lambda i: (i, 0))
    tab = lambda w: pl.BlockSpec((tm, w), lambda i: (0, 0))
    return pl.pallas_call(
        _mla_prep_sample_kernel,
        grid=(n // tm,),
        in_specs=[pl.BlockSpec((tm, SMALL_COLS), lambda i: (i, small_blk)),
                  tab(LANES), tab(LANES), tab(pw), tab(pw),
                  _const_spec(gq.shape), _const_spec(gkv.shape),
                  _const_spec(wq.shape), _const_spec(wuk.shape)],
        out_specs=[row(MLA_HEADS * KV_LORA), row(pw), row(KV_LORA), row(QK_ROPE)],
        out_shape=[jax.ShapeDtypeStruct((n, MLA_HEADS * KV_LORA), BF16),
                   jax.ShapeDtypeStruct((n, pw), BF16),
                   jax.ShapeDtypeStruct((n, KV_LORA), F32), jax.ShapeDtypeStruct((n, QK_ROPE), F32)],
        compiler_params=_cparams("parallel"),
        name="mla_prep_sample",
    )(proj, cos, sin, cosp, sinp, gq, gkv, wq, wuk)


def _flash_kernel(q_ref, k_ref, vt_ref, o_ref, *, tq, tk):
    qi = pl.program_id(2)
    q2 = q_ref[...]
    qs = (q2[:, :LANES], q2[:, LANES:])
    neg = jnp.finfo(F32).min

    def scores(j):
        start = pl.multiple_of(j * tk, tk)
        return tuple(_nt_dot(k_ref[pl.ds(start, tk), hh * LANES:(hh + 1) * LANES], qs[hh])
                     for hh in range(2))

    def col_max(sts):
        return tuple(jnp.max(st, axis=0, keepdims=True) for st in sts)

    def fold(stats, sts, cmax, j):
        start = pl.multiple_of(j * tk, tk)
        new = []
        for hh in range(2):
            m, l, acc = stats[hh]
            vt = vt_ref[hh * V_HEAD:(hh + 1) * V_HEAD, pl.ds(start, tk)]
            m_new = jnp.maximum(m, cmax[hh])
            alpha = jnp.exp2(m - m_new)
            p = jnp.exp2(sts[hh] - m_new)
            l = alpha * l + jnp.sum(p, axis=0, keepdims=True)
            acc = alpha * acc + jnp.dot(vt, p.astype(BF16), preferred_element_type=F32)
            new.append((m_new, l, acc))
        return tuple(new)

    def body(j, carry):
        stats, cmax = carry
        nxt = col_max(scores(jnp.minimum(j + 1, qi - 1)))
        return fold(stats, scores(j), cmax, j), nxt

    assert tk == tq
    one = (jnp.full((1, tq), -jnp.inf, F32), jnp.zeros((1, tq), F32), jnp.zeros((V_HEAD, tq), F32))
    key = lax.broadcasted_iota(jnp.int32, (tk, tq), 0)
    qry = lax.broadcasted_iota(jnp.int32, (tk, tq), 1)
    diag = tuple(jnp.where(key <= qry, st, neg) for st in scores(qi))
    stats = fold((one, one), diag, col_max(diag), qi)
    stats, _ = lax.fori_loop(0, qi, body, (stats, col_max(scores(0))))
    (_, l0, acc0), (_, l1, acc1) = stats
    o_ref[...] = jnp.concatenate([acc0 / l0, acc1 / l1], axis=0).T.astype(o_ref.dtype)


def _prompt_attention(q, k, vt, batch, seq, tq, tk):
    n = batch * seq
    nq = seq // tq
    pairs = MLA_HEADS // 2
    return pl.pallas_call(
        functools.partial(_flash_kernel, tq=tq, tk=tk),
        grid=(batch, pairs, nq),
        in_specs=[pl.BlockSpec((tq, 2 * LANES), lambda b, p, i: (b * nq + i, p)),
                  pl.BlockSpec((seq, 2 * LANES), lambda b, p, i: (b, p)),
                  pl.BlockSpec((None, 2 * V_HEAD, seq), lambda b, p, i: (b, p, 0))],
        out_specs=pl.BlockSpec((tq, 2 * V_HEAD), lambda b, p, i: (b * nq + i, p)),
        out_shape=jax.ShapeDtypeStruct((n, MLA_HEADS * V_HEAD), BF16),
        compiler_params=_cparams("parallel", "parallel", "arbitrary"),
        name="prompt_attention",
    )(q, k, vt)


def _paged_attn_kernel(pt_ref, qlat_ref, qpe_ref, ckvn_ref, kpen_ref, ckv_hbm, kpe_hbm, o_ref,
                       ckv_buf, kpe_buf, sem, m_sc, l_sc, acc_sc, *, pps, sub_pages, t_new, n_chunks):
    step = pl.program_id(0)
    n_steps = pl.num_programs(0)
    c = step % n_chunks
    n_slots = ckv_buf.shape[0]
    ahead = n_slots - 1
    slot = step % n_slots
    rows = t_new * MLA_HEADS

    def page_copies(s, sl):
        out = []
        for k in range(pps):
            page = pt_ref[s * pps + k]
            tok = pl.ds(k * PAGE_SIZE, PAGE_SIZE)
            out.append(pltpu.make_async_copy(ckv_hbm.at[page], ckv_buf.at[sl, tok, :], sem.at[sl]))
            out.append(pltpu.make_async_copy(kpe_hbm.at[page], kpe_buf.at[sl, :, tok], sem.at[sl]))
        return out

    for s0 in range(ahead):
        @pl.when(jnp.logical_and(step == 0, s0 < n_steps))
        def _(s0=s0):
            for cp in page_copies(s0, s0):
                cp.start()

    @pl.when(c == 0)
    def _():
        m_sc[...] = jnp.full(m_sc.shape, -jnp.inf, F32)
        l_sc[...] = jnp.zeros(l_sc.shape, F32)
        acc_sc[...] = jnp.zeros(acc_sc.shape, F32)

    for cp in page_copies(step, slot):
        cp.wait()

    @pl.when(step + ahead < n_steps)
    def _():
        for cp in page_copies(step + ahead, (step + ahead) % n_slots):
            cp.start()

    qlat = qlat_ref[...]
    qpe = qpe_ref[...]
    cks, scores = [], []
    for j in range(pps // sub_pages):
        tok = pl.ds(j * sub_pages * PAGE_SIZE, sub_pages * PAGE_SIZE)
        ck = ckv_buf[slot, tok, :].astype(BF16)
        kp = kpe_buf[slot, :, tok].astype(BF16)
        cks.append(ck)
        scores.append(_nt_dot(qlat, ck) + jnp.dot(qpe, kp, preferred_element_type=F32))
    m_run, l_run, acc = m_sc[...], l_sc[...], acc_sc[...]
    for ck, s in zip(cks, scores):
        m_new = jnp.maximum(m_run, jnp.max(s, axis=-1, keepdims=True))
        alpha = jnp.exp2(m_run - m_new)
        p = jnp.exp2(s - m_new)
        l_run = alpha * l_run + jnp.sum(p, axis=-1, keepdims=True)
        acc = alpha * acc + jnp.dot(p.astype(BF16), ck, preferred_element_type=F32)
        m_run = m_new
    m_sc[...], l_sc[...], acc_sc[...] = m_run, l_run, acc

    @pl.when(c == n_chunks - 1)
    def _():
        qlf = qlat.astype(F32)
        qpf = qpe.astype(F32)
        cn = ckvn_ref[...].astype(BF16).astype(F32)
        kn = kpen_ref[...].astype(BF16).astype(F32)
        tok = lax.broadcasted_iota(jnp.int32, (rows, 1), 0) // MLA_HEADS
        neg = jnp.finfo(F32).min
        s_new = []
        for j in range(t_new):
            sj = (jnp.sum(qlf * cn[j:j + 1, :], axis=-1, keepdims=True)
                  + jnp.sum(qpf * kn[j:j + 1, :], axis=-1, keepdims=True))
            s_new.append(jnp.where(tok >= j, sj, neg))
        m1 = m_sc[...]
        m2 = m1
        for sj in s_new:
            m2 = jnp.maximum(m2, sj)
        a2 = jnp.exp2(m1 - m2)
        l2 = a2 * l_sc[...]
        acc2 = a2 * acc_sc[...]
        for j, sj in enumerate(s_new):
            pj = jnp.exp2(sj - m2)
            l2 = l2 + pj
            acc2 = acc2 + pj.astype(BF16).astype(F32) * cn[j:j + 1, :]
        o_ref[...] = (acc2 / l2).astype(o_ref.dtype)


def _paged_attention(page_table, qlat, qpe, ckv_new, kpe_new, cache_ckv, cache_kpe, t_new, pps):
    dec_batch, n_pages = page_table.shape
    rows = t_new * MLA_HEADS
    n_chunks = n_pages // pps
    by_seq = lambda s, pt: (s // n_chunks, 0)
    by_seq3 = lambda s, pt: (s // n_chunks, 0, 0)
    grid_spec = pltpu.PrefetchScalarGridSpec(
        num_scalar_prefetch=1,
        grid=(dec_batch * n_chunks,),
        in_specs=[pl.BlockSpec((rows, KV_LORA), by_seq),
                  pl.BlockSpec((rows, QK_ROPE), by_seq),
                  pl.BlockSpec((None, t_new, KV_LORA), by_seq3),
                  pl.BlockSpec((None, t_new, QK_ROPE), by_seq3),
                  pl.BlockSpec(memory_space=pl.ANY),
                  pl.BlockSpec(memory_space=pl.ANY)],
        out_specs=pl.BlockSpec((rows, KV_LORA), by_seq),
        scratch_shapes=[pltpu.VMEM((PAGE_RING_SLOTS, pps * PAGE_SIZE, KV_LORA), F32),
                        pltpu.VMEM((PAGE_RING_SLOTS, QK_ROPE, pps * PAGE_SIZE), F32),
                        pltpu.SemaphoreType.DMA((PAGE_RING_SLOTS,)),
                        pltpu.VMEM((rows, 1), F32), pltpu.VMEM((rows, 1), F32),
                        pltpu.VMEM((rows, KV_LORA), F32)],
    )
    return pl.pallas_call(
        functools.partial(_paged_attn_kernel, pps=pps, sub_pages=_tile_rows(pps, SOFTMAX_SUB_PAGES), t_new=t_new,
                          n_chunks=n_chunks),
        grid_spec=grid_spec,
        out_shape=jax.ShapeDtypeStruct((dec_batch * rows, KV_LORA), BF16),
        compiler_params=_cparams("arbitrary"),
        name="paged_attention",
    )(page_table.reshape(-1), qlat, qpe, ckv_new, kpe_new, cache_ckv, cache_kpe)


def _uv_proj_kernel(olat_ref, wuv_ref, o_ref):
    for h in range(MLA_HEADS):
        o_ref[:, h * V_HEAD:(h + 1) * V_HEAD] = jnp.dot(
            olat_ref[:, h * KV_LORA:(h + 1) * KV_LORA], wuv_ref[h],
            preferred_element_type=F32).astype(o_ref.dtype)


def _uv_proj(olat, wuv, tm):
    n = olat.shape[0]
    return pl.pallas_call(
        _uv_proj_kernel,
        grid=(n // tm,),
        in_specs=[pl.BlockSpec((tm, MLA_HEADS * KV_LORA), lambda i: (i, 0)), _const_spec(wuv.shape)],
        out_specs=pl.BlockSpec((tm, MLA_HEADS * V_HEAD), lambda i: (i, 0)),
        out_shape=jax.ShapeDtypeStruct((n, MLA_HEADS * V_HEAD), BF16),
        compiler_params=_cparams("parallel"),
        name="uv_proj",
    )(olat, wuv)


def _split3(x):
    hi = x.astype(BF16)
    r = x - hi.astype(F32)
    mid = r.astype(BF16)
    lo = (r - mid.astype(F32)).astype(BF16)
    return hi, mid, lo


def _expand_heads(v, expand_ref):
    hi = v.astype(BF16)
    mid = (v - hi.astype(F32)).astype(BF16)
    return jnp.dot(jnp.concatenate([hi, mid], axis=1), expand_ref[...], preferred_element_type=F32)


def _ssd_heads_wide(xc, dt, a_cum, a_cum_t, exp_cum, dte, tril, dskip_ref, expand_ref, y_ref, state_t):
    chunk = xc.shape[0]
    gw = HEADS_PER_GROUP * SSM_HEADDIM
    b_off = D_INNER
    c_off = D_INNER + SSM_GROUPS * SSM_STATE
    xs = xc[:, :D_INNER]
    dt_e = _expand_heads(dt, expand_ref)
    exp_cum_e = _expand_heads(exp_cum, expand_ref)
    dte_e = _expand_heads(dte, expand_ref)
    xdt = xs * dt_e
    xdt_b = xdt.astype(BF16)
    xdte_b = (xdt * dte_e).astype(BF16)
    skip = xs * dskip_ref[...]
    lane = lax.broadcasted_iota(jnp.int32, (chunk, LANES), 1)
    for g in range(SSM_GROUPS):
        gcols = slice(g * gw, (g + 1) * gw)
        bg = xc[:, b_off + g * SSM_STATE:b_off + (g + 1) * SSM_STATE].astype(BF16)
        cg = xc[:, c_off + g * SSM_STATE:c_off + (g + 1) * SSM_STATE].astype(BF16)
        cb = _nt_dot(cg, bg)
        st_g = state_t[:, gcols]
        y_off = jnp.dot(cg, st_g.astype(BF16), preferred_element_type=F32)
        for pair in range(HEADS_PER_GROUP // 2):
            h0 = g * HEADS_PER_GROUP + 2 * pair
            cols = slice(h0 * SSM_HEADDIM, (h0 + 2) * SSM_HEADDIM)
            rhs = xdt_b[:, cols]
            ys = []
            for h in (h0, h0 + 1):
                seg = a_cum[:, h:h + 1] - a_cum_t[h:h + 1, :]
                decay = jnp.exp(jnp.where(tril, seg, -jnp.inf))
                ys.append(jnp.dot((cb * decay).astype(BF16), rhs, preferred_element_type=F32))
            y = jnp.where(lane < SSM_HEADDIM, ys[0], ys[1])
            y_ref[:, cols] = (y + y_off[:, pair * LANES:(pair + 1) * LANES] * exp_cum_e[:, cols]
                              + skip[:, cols])
        cs_t = lax.dot_general(bg, xdte_b[:, gcols], (((0,), (0,)), ((), ())),
                               preferred_element_type=F32)
        state_t[:, gcols] = st_g * exp_cum_e[chunk - 1:chunk, gcols] + cs_t


def _ssd_kernel(*refs, l_in, chunk, has_init):
    if has_init:
        (xbc_ref, dt_ref, conv0_ref, ssm0_ref, cw_ref, cb_ref, dtb_ref, alog_ref, dskip_ref,
         y_ref, ssm_ref, conv_ref, xpad, state, dtpad) = refs
    else:
        (xbc_ref, dt_ref, cw_ref, cb_ref, dtb_ref, alog_ref, dskip_ref, expand_ref,
         y_ref, ssm_ref, conv_ref, xpad, state, dtpad) = refs
    c = pl.program_id(1)
    last = pl.num_programs(1) - 1
    tail = SUBLANES

    @pl.when(c == 0)
    def _():
        if has_init:
            xpad[0:tail, :] = conv0_ref[...]
            state[...] = ssm0_ref[...]
        else:
            xpad[0:tail, :] = jnp.zeros((tail, CONV_DIM), F32)
            state[...] = jnp.zeros(state.shape, F32)
        if l_in < chunk:
            xpad[tail:, :] = jnp.zeros((chunk, CONV_DIM), F32)

    xpad[tail:tail + l_in, :] = xbc_ref[...]

    conv = cb_ref[...] + cw_ref[CONV_WIDTH - 1:CONV_WIDTH, :] * xpad[tail:tail + chunk, :]
    for k in range(CONV_WIDTH - 1):
        off = tail - (CONV_WIDTH - 1) + k
        conv = conv + cw_ref[k:k + 1, :] * xpad[off:off + chunk, :]
    xc = _silu(conv)

    win = tail + ((l_in - (CONV_WIDTH - 1)) // SUBLANES) * SUBLANES
    sub = (l_in - (CONV_WIDTH - 1)) % SUBLANES

    @pl.when(c == last)
    def _():
        conv_ref[...] = xpad[win:win + SUBLANES, :][sub:sub + CONV_WIDTH - 1, :]

    xpad[0:tail, :] = xpad[chunk:chunk + tail, :]

    rows = lax.broadcasted_iota(jnp.int32, (chunk, LANES), 0)
    if l_in < chunk:
        dtpad[...] = jnp.zeros((chunk, LANES), F32)
        dtpad[0:l_in, :] = dt_ref[...]
        dt_raw = dtpad[...]
    else:
        dt_raw = dt_ref[...]
    dt = jnp.where(rows < l_in, _softplus(dt_raw + dtb_ref[...]), 0.0)
    a = -jnp.exp(alog_ref[...])
    da = dt * a

    ti = lax.broadcasted_iota(jnp.int32, (chunk, chunk), 0)
    si = lax.broadcasted_iota(jnp.int32, (chunk, chunk), 1)
    tril = ti >= si
    tril_b = tril.astype(BF16)
    hi, mid, lo = _split3(da)
    cum3 = jnp.dot(tril_b, jnp.concatenate([hi, mid, lo], axis=1), preferred_element_type=F32)
    a_cum = (cum3[:, 0:LANES] + cum3[:, LANES:2 * LANES]) + cum3[:, 2 * LANES:3 * LANES]
    a_cum_t = a_cum.T
    exp_cum = jnp.exp(a_cum)
    a_last = a_cum[chunk - 1:chunk, :]
    dte = jnp.exp(a_last - a_cum)

    gw = HEADS_PER_GROUP * SSM_HEADDIM
    b_off = D_INNER
    c_off = D_INNER + SSM_GROUPS * SSM_STATE

    if not has_init:
        assert l_in == chunk
        _ssd_heads_wide(xc, dt, a_cum, a_cum_t, exp_cum, dte, tril, dskip_ref, expand_ref, y_ref, state)

        @pl.when(c == last)
        def _():
            ssm_ref[...] = state[...].T
        return

    chunk_decay_t = jnp.exp(a_cum_t[:, chunk - 1:chunk])
    for g in range(SSM_GROUPS):
        bg = xc[:, b_off + g * SSM_STATE:b_off + (g + 1) * SSM_STATE].astype(BF16)
        cg = xc[:, c_off + g * SSM_STATE:c_off + (g + 1) * SSM_STATE].astype(BF16)
        cb = _nt_dot(cg, bg)
        st_g = state[g * gw:(g + 1) * gw, :]
        y_off = _nt_dot(cg, st_g.astype(BF16))
        xdte_parts = []
        for hl in range(HEADS_PER_GROUP):
            h = g * HEADS_PER_GROUP + hl
            xs = xc[:, h * SSM_HEADDIM:(h + 1) * SSM_HEADDIM]
            xdt = xs * dt[:, h:h + 1]
            seg = a_cum[:, h:h + 1] - a_cum_t[h:h + 1, :]
            decay = jnp.exp(jnp.where(tril, seg, -jnp.inf))
            y = jnp.dot((cb * decay).astype(BF16), xdt.astype(BF16), preferred_element_type=F32)
            y = y + y_off[:, hl * SSM_HEADDIM:(hl + 1) * SSM_HEADDIM] * exp_cum[:, h:h + 1]
            y = y + xs * dskip_ref[:, h * SSM_HEADDIM:(h + 1) * SSM_HEADDIM]
            y_ref[:, h * SSM_HEADDIM:(h + 1) * SSM_HEADDIM] = y[0:l_in, :]
            xdte_parts.append(xdt * dte[:, h:h + 1])
        xdte_t = jnp.concatenate(xdte_parts, axis=1).T.astype(BF16)
        cs = jnp.dot(xdte_t, bg, preferred_element_type=F32)
        for hl in range(HEADS_PER_GROUP):
            h = g * HEADS_PER_GROUP + hl
            r0 = g * gw + hl * SSM_HEADDIM
            state[r0:r0 + SSM_HEADDIM, :] = (st_g[hl * SSM_HEADDIM:(hl + 1) * SSM_HEADDIM, :]
                                             * chunk_decay_t[h:h + 1, :]
                                             + cs[hl * SSM_HEADDIM:(hl + 1) * SSM_HEADDIM, :])

    @pl.when(c == last)
    def _():
        ssm_ref[...] = state[...]


def _ssd(proj3, conv0, ssm0, cw, cb, dtb, alog, dskip, l_in, chunk):
    batch, seq, _ = proj3.shape
    nc = seq // l_in
    assert l_in == chunk or nc == 1
    has_init = conv0 is not None
    hp = SSM_HEADS * SSM_HEADDIM
    dt_blk = (PROJ_SMALL + SMALL_DT) // LANES
    in_specs = [pl.BlockSpec((None, l_in, CONV_DIM), lambda b, c: (b, c, PROJ_XBC // CONV_DIM)),
                pl.BlockSpec((None, l_in, LANES), lambda b, c: (b, c, dt_blk))]
    args = [proj3, proj3]
    if has_init:
        in_specs += [pl.BlockSpec((None, SUBLANES, CONV_DIM), lambda b, c: (b, 0, 0)),
                     pl.BlockSpec((None, hp, SSM_STATE), lambda b, c: (b, 0, 0))]
        args += [conv0, ssm0]
    consts = [cw, cb, dtb, alog, dskip]
    if not has_init:
        sel = (jnp.arange(LANES)[:, None] == jnp.arange(hp)[None, :] // SSM_HEADDIM).astype(BF16)
        consts.append(jnp.concatenate([sel, sel], axis=0))
    state_shape = (hp, SSM_STATE) if has_init else (SSM_STATE, hp)
    in_specs += [_const_spec(x.shape) for x in consts]
    args += consts
    return pl.pallas_call(
        functools.partial(_ssd_kernel, l_in=l_in, chunk=chunk, has_init=has_init),
        grid=(batch, nc),
        in_specs=in_specs,
        out_specs=[pl.BlockSpec((None, l_in, D_INNER), lambda b, c: (b, c, 0)),
                   pl.BlockSpec((None, hp, SSM_STATE), lambda b, c: (b, 0, 0)),
                   pl.BlockSpec((None, CONV_WIDTH - 1, CONV_DIM), lambda b, c: (b, 0, 0))],
        out_shape=[jax.ShapeDtypeStruct((batch, seq, D_INNER), F32),
                   jax.ShapeDtypeStruct((batch, hp, SSM_STATE), F32),
                   jax.ShapeDtypeStruct((batch, CONV_WIDTH - 1, CONV_DIM), F32)],
        scratch_shapes=[pltpu.VMEM((SUBLANES + chunk, CONV_DIM), F32), pltpu.VMEM(state_shape, F32),
                        pltpu.VMEM((chunk, LANES), F32)],
        compiler_params=_cparams("parallel", "arbitrary"),
        name="conv_ssd",
    )(*args)


def _layer_norm(x, g, b):
    mu = jnp.mean(x, axis=-1, keepdims=True)
    xc = x - mu
    var = jnp.mean(xc * xc, axis=-1, keepdims=True)
    return xc * lax.rsqrt(var + 1e-5) * g + b


def _merge_kernel(x_ref, attn_ref, y_ref, z_ref, g_ref, bg_ref, ng_ref, wa_ref, wb_ref, wo_ref,
                  l1g_ref, l1b_ref, h_ref):
    gsz = D_INNER // SSM_GROUPS
    yg = y_ref[...] * _silu(z_ref[...])
    parts = []
    for g in range(SSM_GROUPS):
        parts.append(_rms(yg[:, g * gsz:(g + 1) * gsz], ng_ref[:, g * gsz:(g + 1) * gsz], 1e-5).astype(BF16))
    ygn = jnp.concatenate(parts, axis=1)
    br_a = jnp.dot(attn_ref[...], wa_ref[...], preferred_element_type=F32)
    br_b = jnp.dot(ygn, wb_ref[...], preferred_element_type=F32)
    gates = _sigmoid(g_ref[...] + bg_ref[...])
    merged = gates[:, :D_MODEL] * br_a + gates[:, D_MODEL:] * br_b
    mix = jnp.dot(merged.astype(BF16), wo_ref[...], preferred_element_type=F32)
    h_ref[...] = _layer_norm(ALPHA * x_ref[...] + mix, l1g_ref[...], l1b_ref[...])


def _merge(x, attn, y, proj, bg, ng, wa, wb, wo, l1g, l1b, tm):
    n = x.shape[0]
    row = lambda w: pl.BlockSpec((tm, w), lambda i: (i, 0))
    consts = [bg, ng, wa, wb, wo, l1g, l1b]
    return pl.pallas_call(
        _merge_kernel,
        grid=(n // tm,),
        in_specs=[row(D_MODEL), row(MLA_HEADS * V_HEAD), row(D_INNER),
                  pl.BlockSpec((tm, D_INNER), lambda i: (i, PROJ_Z // D_INNER)),
                  pl.BlockSpec((tm, 2 * D_MODEL), lambda i: (i, PROJ_G // (2 * D_MODEL)))]
                 + [_const_spec(c.shape) for c in consts],
        out_specs=row(D_MODEL),
        out_shape=jax.ShapeDtypeStruct((n, D_MODEL), F32),
        compiler_params=_cparams("parallel"),
        name="merge_ln1",
    )(x, attn, y, proj, proj, *consts)


def _ffn_kernel(h_ref, wg_ref, wu_ref, wd_ref, l2g_ref, l2b_ref, o_ref):
    h = h_ref[...]
    hb = h.astype(BF16)
    g = jnp.dot(hb, wg_ref[...], preferred_element_type=F32)
    u = jnp.dot(hb, wu_ref[...], preferred_element_type=F32)
    f = jnp.dot((_silu(g) * u).astype(BF16), wd_ref[...], preferred_element_type=F32)
    o_ref[...] = _layer_norm(ALPHA * h + f, l2g_ref[...], l2b_ref[...])


def _ffn(h, wg, wu, wd, l2g, l2b, tm):
    n = h.shape[0]
    row = pl.BlockSpec((tm, D_MODEL), lambda i: (i, 0))
    consts = [wg, wu, wd, l2g, l2b]
    return pl.pallas_call(
        _ffn_kernel,
        grid=(n // tm,),
        in_specs=[row] + [_const_spec(c.shape) for c in consts],
        out_specs=row,
        out_shape=jax.ShapeDtypeStruct((n, D_MODEL), F32),
        compiler_params=_cparams("parallel"),
        name="ffn_ln2",
    )(h, *consts)


def _rot_half(w):
    half = QK_ROPE // 2
    return jnp.concatenate([-w[..., half:], w[..., :half]], axis=-1)


def _prep_weights(w_in, w_uq, w_ukv):
    wq, wkv, wkpe, wz, wxbc, wdt, wg = jnp.split(w_in.T.astype(BF16), IN_OFFSETS, axis=0)
    zr = lambda n: jnp.zeros((n, D_MODEL), BF16)
    pad_hi = LANES - PE_LANE0 - QK_ROPE
    half = QK_ROPE // 2
    krot = jnp.concatenate([-wkpe[half:], wkpe[:half]], axis=0)
    small = jnp.concatenate([wq, wkv, zr(PE_LANE0), wkpe, zr(pad_hi), zr(PE_LANE0), krot, zr(pad_hi),
                             wdt, zr(LANES - SSM_HEADS)], axis=0)
    w_in_p = jnp.concatenate([wxbc, small, wz, wg], axis=0)

    uq = w_uq.reshape(Q_LORA, MLA_HEADS, QK_NOPE + QK_ROPE)
    nope, pe = uq[..., :QK_NOPE], uq[..., QK_NOPE:]
    rot = _rot_half(pe)
    zq = lambda n: jnp.zeros((Q_LORA, MLA_HEADS, n), F32)
    hw = MLA_HEADS * LANES
    wq_a = jnp.concatenate([nope, pe, zq(pad_hi)], axis=-1).reshape(Q_LORA, hw)
    wq_b = jnp.concatenate([zq(PE_LANE0), rot, zq(pad_hi)], axis=-1).reshape(Q_LORA, hw)
    wq_prompt = jnp.concatenate([wq_a, wq_b], axis=1).astype(BF16)
    wq_n = jnp.concatenate([nope, zq(LANES - QK_NOPE)], axis=-1).reshape(Q_LORA, hw)
    wq_sample = jnp.concatenate([wq_n, pe.reshape(Q_LORA, -1), rot.reshape(Q_LORA, -1)], axis=1).astype(BF16)

    ukv = w_ukv.reshape(KV_LORA, MLA_HEADS, QK_NOPE + V_HEAD)
    k_nope, v = ukv[..., :QK_NOPE], ukv[..., QK_NOPE:]
    zk = jnp.zeros((KV_LORA, MLA_HEADS, LANES - QK_NOPE), F32)
    wkv_prompt = jnp.concatenate([jnp.concatenate([k_nope, zk], axis=-1).reshape(KV_LORA, hw),
                                  v.reshape(KV_LORA, -1)], axis=1).astype(BF16)
    wuk_t = jnp.concatenate([k_nope, zk], axis=-1).transpose(1, 2, 0).astype(BF16)
    wuv = v.transpose(1, 0, 2).astype(BF16)
    return w_in_p, wq_prompt, wq_sample, wkv_prompt, wuk_t, wuv


def _rope_tables(pos):
    half = QK_ROPE // 2
    inv = ROPE_THETA ** (-jnp.arange(0, QK_ROPE, 2, dtype=F32) / QK_ROPE)
    ang = pos.astype(F32)[:, None] * inv[None, :]
    c2 = jnp.concatenate([jnp.cos(ang), jnp.cos(ang)], axis=1)
    s2 = jnp.concatenate([jnp.sin(ang), jnp.sin(ang)], axis=1)
    t = pos.shape[0]
    cos = jnp.concatenate([jnp.ones((t, PE_LANE0), F32), c2, jnp.zeros((t, LANES - PE_LANE0 - QK_ROPE), F32)], axis=1)
    sin = jnp.concatenate([jnp.zeros((t, PE_LANE0), F32), s2, jnp.zeros((t, LANES - PE_LANE0 - QK_ROPE), F32)], axis=1)
    return cos, sin, jnp.tile(c2, (1, MLA_HEADS)), jnp.tile(s2, (1, MLA_HEADS))


def _tile_rows(n, pref):
    t = min(pref, n)
    while n % t:
        t //= 2
    return t


def kernel(x_prompt, x_sample, cache_ckv, cache_kpe, state_ssm, state_conv, page_table, w_in, b_gate, q_norm_g, kv_norm_g, w_uq, w_ukv, conv_w, conv_b, dt_bias, a_log, d_skip, ssm_norm_g, w_branch_a, w_branch_b, w_out, ln1_g, ln1_b, w_ffn_gate, w_ffn_up, w_ffn_down, ln2_g, ln2_b):
    assert w_in.shape[0] == DEPTH == 1
    batch, seq, _ = x_prompt.shape
    dec_batch, t_new, _ = x_sample.shape
    n_pages = page_table.shape[1]
    past_len = n_pages * PAGE_SIZE
    n_p = batch * seq
    n_s = dec_batch * t_new

    w_in_p, wq_prompt, wq_sample, wkv_prompt, wuk_t, wuv = _prep_weights(w_in[0], w_uq[0], w_ukv[0])
    row2 = lambda v: v.reshape(1, -1).astype(F32)
    gq, gkv = row2(q_norm_g[0]), row2(kv_norm_g[0])
    pad_heads = lambda v: jnp.concatenate([v.astype(F32), jnp.zeros((LANES - SSM_HEADS,), F32)]).reshape(1, LANES)
    dtb, alog = pad_heads(dt_bias[0]), pad_heads(a_log[0])
    dskip = jnp.repeat(d_skip[0].astype(F32), SSM_HEADDIM).reshape(1, D_INNER)
    cw, cb = conv_w[0].astype(F32), row2(conv_b[0])
    bg, ng = row2(b_gate[0]), row2(ssm_norm_g[0])
    wa, wb, wo = w_branch_a[0].astype(BF16), w_branch_b[0].astype(BF16), w_out[0].astype(BF16)
    wfg, wfu, wfd = w_ffn_gate[0].astype(BF16), w_ffn_up[0].astype(BF16), w_ffn_down[0].astype(BF16)
    l1g, l1b, l2g, l2b = row2(ln1_g[0]), row2(ln1_b[0]), row2(ln2_g[0]), row2(ln2_b[0])

    xp = x_prompt.reshape(n_p, D_MODEL)
    proj_p = _matmul(xp, w_in_p, _tile_rows(n_p, IN_PROJ_ROWS), IN_PROJ_COL_TILE, F32)
    cos_p, sin_p, _, _ = _rope_tables(jnp.arange(seq))
    q_p, k_p, vt_p, ckv_p, kpe_p = _mla_prep_prompt(proj_p, cos_p, sin_p, gq, gkv, wq_prompt, wkv_prompt,
                                                   _tile_rows(seq, MLA_PREP_ROWS))
    tq_p = _tile_rows(seq, ATTN_TILE)
    attn_p = _prompt_attention(q_p, k_p, vt_p, batch, seq, tq_p, tq_p)
    chunk_p = min(SSD_CHUNK, seq)
    y_p, ssm_p, conv_p = _ssd(proj_p.reshape(batch, seq, PROJ_COLS), None, None, cw, cb, dtb, alog, dskip,
                              chunk_p, chunk_p)
    h_p = _merge(xp, attn_p, y_p.reshape(n_p, D_INNER), proj_p, bg, ng, wa, wb, wo, l1g, l1b,
                 _tile_rows(n_p, DENSE_ROWS))
    out_p = _ffn(h_p, wfg, wfu, wfd, l2g, l2b, _tile_rows(n_p, DENSE_ROWS))

    xs = x_sample.reshape(n_s, D_MODEL)
    proj_s = _matmul(xs, w_in_p, _tile_rows(n_s, SAMPLE_PROJ_ROWS), IN_PROJ_COL_TILE, F32)
    cos_s, sin_s, cosp_s, sinp_s = _rope_tables(past_len + jnp.arange(t_new))
    tm_s = t_new * _tile_rows(dec_batch, SAMPLE_PREP_SEQS)
    rep = lambda t: jnp.tile(t, (tm_s // t_new, 1))
    qlat_s, qpe_s, ckv_s, kpe_s = _mla_prep_sample(proj_s, rep(cos_s), rep(sin_s), rep(cosp_s), rep(sinp_s),
                                                   gq, gkv, wq_sample, wuk_t, tm_s)
    pps = _tile_rows(n_pages, PAGES_PER_STEP)
    kpe_t = jnp.swapaxes(cache_kpe.reshape(cache_kpe.shape[1:]), 1, 2)
    olat_s = _paged_attention(page_table, qlat_s.reshape(n_s * MLA_HEADS, KV_LORA),
                              qpe_s.reshape(n_s * MLA_HEADS, QK_ROPE),
                              ckv_s.reshape(dec_batch, t_new, KV_LORA), kpe_s.reshape(dec_batch, t_new, QK_ROPE),
                              cache_ckv.reshape(cache_ckv.shape[1:]), kpe_t, t_new, pps)
    attn_s = _uv_proj(olat_s.reshape(n_s, MLA_HEADS * KV_LORA), wuv, _tile_rows(n_s, DENSE_ROWS))
    conv0 = jnp.concatenate([jnp.zeros((dec_batch, SUBLANES - (CONV_WIDTH - 1), CONV_DIM), F32),
                             state_conv[0].astype(F32)], axis=1)
    ssm0 = state_ssm.astype(F32).reshape(dec_batch, SSM_HEADS * SSM_HEADDIM, SSM_STATE)
    y_s, ssm_s, conv_s = _ssd(proj_s.reshape(dec_batch, t_new, PROJ_COLS), conv0, ssm0, cw, cb, dtb, alog, dskip,
                              t_new, SUBLANES * pl.cdiv(t_new, SUBLANES))
    h_s = _merge(xs, attn_s, y_s.reshape(n_s, D_INNER), proj_s, bg, ng, wa, wb, wo, l1g, l1b,
                 _tile_rows(n_s, DENSE_ROWS))
    out_s = _ffn(h_s, wfg, wfu, wfd, l2g, l2b, _tile_rows(n_s, DENSE_ROWS))

    st_shape = (SSM_HEADS, SSM_HEADDIM, SSM_STATE)
    return (out_p.reshape(batch, seq, D_MODEL), out_s.reshape(dec_batch, t_new, D_MODEL),
            ckv_p.reshape(1, batch, seq, KV_LORA), kpe_p.reshape(1, batch, seq, QK_ROPE),
            ssm_p.reshape(1, batch, *st_shape), conv_p.reshape(1, batch, CONV_WIDTH - 1, CONV_DIM),
            ckv_s.reshape(1, dec_batch, t_new, KV_LORA), kpe_s.reshape(1, dec_batch, t_new, QK_ROPE),
            ssm_s.reshape(1, dec_batch, *st_shape), conv_s.reshape(1, dec_batch, CONV_WIDTH - 1, CONV_DIM))
```

```python
import functools
import math

import jax
import jax.numpy as jnp
from jax import lax
from jax.experimental import pallas as pl
from jax.experimental.pallas import tpu as pltpu

F32 = jnp.float32
BF16 = jnp.bfloat16

D_MODEL = 1024
MLA_HEADS = 16
QK_NOPE = 64
QK_ROPE = 32
V_HEAD = 64
Q_LORA = 384
KV_LORA = 256
ROPE_THETA = 10000.0
ATTN_SCALE = (QK_NOPE + QK_ROPE) ** -0.5
Q_PRESCALE = ATTN_SCALE * math.log2(math.e)
D_INNER = 2 * D_MODEL
SSM_HEADDIM = 64
SSM_HEADS = D_INNER // SSM_HEADDIM
SSM_GROUPS = 4
HEADS_PER_GROUP = SSM_HEADS // SSM_GROUPS
SSM_STATE = 128
CONV_WIDTH = 4
CONV_DIM = D_INNER + 2 * SSM_GROUPS * SSM_STATE
SSD_CHUNK = 256
D_FF = ((8 * D_MODEL // 3 + 255) // 256) * 256
PAGE_SIZE = 128
DEPTH = 1
ALPHA = (2.0 * DEPTH) ** 0.25
IN_SPLITS = (Q_LORA, KV_LORA, QK_ROPE, D_INNER, CONV_DIM, SSM_HEADS, 2 * D_MODEL)
IN_OFFSETS = tuple(sum(IN_SPLITS[:i + 1]) for i in range(len(IN_SPLITS) - 1))

LANES = 128
SUBLANES = 8
VMEM_LIMIT_BYTES = 56 * 1024 * 1024
PAGE_RING_SLOTS = 3

IN_PROJ_ROWS = 2048
IN_PROJ_COL_TILE = 1024
SAMPLE_PROJ_ROWS = 512
MLA_PREP_ROWS = 512
SAMPLE_PREP_SEQS = 64
ATTN_TILE = 512
DENSE_ROWS = 256
PAGES_PER_STEP = 64
SOFTMAX_SUB_PAGES = 16

PROJ_XBC = 0
PROJ_SMALL = CONV_DIM
PROJ_Z = 4096
PROJ_G = 6144
PROJ_COLS = 8192
SMALL_CQ = 0
SMALL_CKV = Q_LORA
SMALL_KPE = Q_LORA + KV_LORA
SMALL_KROT = SMALL_KPE + LANES
SMALL_DT = SMALL_KROT + LANES
SMALL_COLS = 1024
PE_LANE0 = QK_NOPE


def _cparams(*sem):
    return pltpu.CompilerParams(dimension_semantics=sem, vmem_limit_bytes=VMEM_LIMIT_BYTES)


def _nt_dot(a, b):
    return lax.dot_general(a, b, (((1,), (1,)), ((), ())), preferred_element_type=F32)


def _silu(x):
    return x * (1.0 / (1.0 + jnp.exp(-x)))


def _sigmoid(x):
    return 1.0 / (1.0 + jnp.exp(-x))


def _softplus(x):
    return jnp.maximum(x, 0.0) + jnp.log(1.0 + jnp.exp(-jnp.abs(x)))


def _matmul_kernel(x_ref, wt_ref, o_ref):
    o_ref[...] = _nt_dot(x_ref[...].astype(BF16), wt_ref[...]).astype(o_ref.dtype)


def _matmul(x, wt, tm, tn, out_dtype):
    m, k = x.shape
    n = wt.shape[0]
    return pl.pallas_call(
        _matmul_kernel,
        grid=(m // tm, n // tn),
        in_specs=[pl.BlockSpec((tm, k), lambda i, j: (i, 0)),
                  pl.BlockSpec((tn, k), lambda i, j: (j, 0))],
        out_specs=pl.BlockSpec((tm, tn), lambda i, j: (i, j)),
        out_shape=jax.ShapeDtypeStruct((m, n), out_dtype),
        compiler_params=_cparams("parallel", "arbitrary"),
        name="in_proj",
    )(x, wt)


def _rms(x, g, eps):
    return x * lax.rsqrt(jnp.mean(x * x, axis=-1, keepdims=True) + eps) * g


def _mla_common(blk, cos, sin, gq_ref, gkv_ref, ckv_ref, kpe_ref):
    cqn = _rms(blk[:, SMALL_CQ:SMALL_CQ + Q_LORA], gq_ref[...], 1e-6).astype(BF16)
    ckvn = _rms(blk[:, SMALL_CKV:SMALL_CKV + KV_LORA], gkv_ref[...], 1e-6)
    ckv_ref[...] = ckvn
    kr = blk[:, SMALL_KPE:SMALL_KPE + LANES] * cos + blk[:, SMALL_KROT:SMALL_KROT + LANES] * sin
    kpe_ref[...] = kr[:, PE_LANE0:PE_LANE0 + QK_ROPE]
    return cqn, ckvn, kr


def _mla_prep_prompt_kernel(blk_ref, cos_ref, sin_ref, gq_ref, gkv_ref, wq_ref, wkv_ref,
                            q_ref, k_ref, vt_ref, ckv_ref, kpe_ref):
    cos = cos_ref[...]
    sin = sin_ref[...]
    cqn, ckvn, kr = _mla_common(blk_ref[...], cos, sin, gq_ref, gkv_ref, ckv_ref, kpe_ref)
    hw = MLA_HEADS * LANES
    qab = jnp.dot(cqn, wq_ref[...], preferred_element_type=F32)
    kv = jnp.dot(ckvn.astype(BF16), wkv_ref[...], preferred_element_type=F32)
    for h in range(MLA_HEADS):
        sl = slice(h * LANES, (h + 1) * LANES)
        q_ref[:, sl] = ((qab[:, sl] * cos + qab[:, hw + h * LANES:hw + (h + 1) * LANES] * sin)
                        * Q_PRESCALE).astype(BF16)
        k_ref[:, sl] = (kv[:, sl] + kr).astype(BF16)
    vt_ref[...] = kv[:, hw:].T.astype(BF16)


def _mla_prep_sample_kernel(blk_ref, cos_ref, sin_ref, cosp_ref, sinp_ref, gq_ref, gkv_ref,
                            wq_ref, wuk_ref, qlat_ref, qpe_ref, ckv_ref, kpe_ref):
    cqn, _, _ = _mla_common(blk_ref[...], cos_ref[...], sin_ref[...], gq_ref, gkv_ref, ckv_ref, kpe_ref)
    hw = MLA_HEADS * LANES
    pw = MLA_HEADS * QK_ROPE
    qall = jnp.dot(cqn, wq_ref[...], preferred_element_type=F32)
    qpe_ref[...] = ((qall[:, hw:hw + pw] * cosp_ref[...] + qall[:, hw + pw:] * sinp_ref[...])
                    * Q_PRESCALE).astype(BF16)
    for h in range(MLA_HEADS):
        qn = qall[:, h * LANES:(h + 1) * LANES].astype(BF16)
        qlat_ref[:, h * KV_LORA:(h + 1) * KV_LORA] = (jnp.dot(
            qn, wuk_ref[h], preferred_element_type=F32) * Q_PRESCALE).astype(BF16)


def _const_spec(shape):
    nd = len(shape)
    return pl.BlockSpec(shape, lambda *_: (0,) * nd)


def _mla_prep_prompt(proj, cos, sin, gq, gkv, wq, wkv, tm):
    n = proj.shape[0]
    hw = MLA_HEADS * LANES
    small_blk = PROJ_SMALL // SMALL_COLS
    row = lambda w: pl.BlockSpec((tm, w), lambda i: (i, 0))
    period = cos.shape[0] // tm
    tab = lambda w: pl.BlockSpec((tm, w), lambda i: (i % period, 0))
    return pl.pallas_call(
        _mla_prep_prompt_kernel,
        grid=(n // tm,),
        in_specs=[pl.BlockSpec((tm, SMALL_COLS), lambda i: (i, small_blk)),
                  tab(LANES), tab(LANES),
                  _const_spec(gq.shape), _const_spec(gkv.shape),
                  _const_spec(wq.shape), _const_spec(wkv.shape)],
        out_specs=[row(hw), row(hw),
                   pl.BlockSpec((None, MLA_HEADS * V_HEAD, tm), lambda i: (i // period, 0, i % period)),
                   row(KV_LORA), row(QK_ROPE)],
        out_shape=[jax.ShapeDtypeStruct((n, hw), BF16), jax.ShapeDtypeStruct((n, hw), BF16),
                   jax.ShapeDtypeStruct((n // cos.shape[0], MLA_HEADS * V_HEAD, cos.shape[0]), BF16),
                   jax.ShapeDtypeStruct((n, KV_LORA), F32), jax.ShapeDtypeStruct((n, QK_ROPE), F32)],
        compiler_params=_cparams("parallel"),
        name="mla_prep_prompt",
    )(proj, cos, sin, gq, gkv, wq, wkv)


def _mla_prep_sample(proj, cos, sin, cosp, sinp, gq, gkv, wq, wuk, tm):
    n = proj.shape[0]
    pw = MLA_HEADS * QK_ROPE
    small_blk = PROJ_SMALL // SMALL_COLS
    row = lambda w: pl.BlockSpec((tm, w), lambda i: (i, 0))
    tab = lambda w: pl.BlockSpec((tm, w), lambda i: (0, 0))
    return pl.pallas_call(
        _mla_prep_sample_kernel,
        grid=(n // tm,),
        in_specs=[pl.BlockSpec((tm, SMALL_COLS), lambda i: (i, small_blk)),
                  tab(LANES), tab(LANES), tab(pw), tab(pw),
                  _const_spec(gq.shape), _const_spec(gkv.shape),
                  _const_spec(wq.shape), _const_spec(wuk.shape)],
        out_specs=[row(MLA_HEADS * KV_LORA), row(pw), row(KV_LORA), row(QK_ROPE)],
        out_shape=[jax.ShapeDtypeStruct((n, MLA_HEADS * KV_LORA), BF16),
                   jax.ShapeDtypeStruct((n, pw), BF16),
                   jax.ShapeDtypeStruct((n, KV_LORA), F32), jax.ShapeDtypeStruct((n, QK_ROPE), F32)],
        compiler_params=_cparams("parallel"),
        name="mla_prep_sample",
    )(proj, cos, sin, cosp, sinp, gq, gkv, wq, wuk)


def _flash_kernel(q_ref, k_ref, vt_ref, o_ref, *, tq, tk):
    qi = pl.program_id(2)
    q2 = q_ref[...]
    qs = (q2[:, :LANES], q2[:, LANES:])
    neg = jnp.finfo(F32).min

    def scores(j):
        start = pl.multiple_of(j * tk, tk)
        return tuple(_nt_dot(k_ref[pl.ds(start, tk), hh * LANES:(hh + 1) * LANES], qs[hh])
                     for hh in range(2))

    def col_max(sts):
        return tuple(jnp.max(st, axis=0, keepdims=True) for st in sts)

    def fold(stats, sts, cmax, j):
        start = pl.multiple_of(j * tk, tk)
        new = []
        for hh in range(2):
            m, l, acc = stats[hh]
            vt = vt_ref[hh * V_HEAD:(hh + 1) * V_HEAD, pl.ds(start, tk)]
            m_new = jnp.maximum(m, cmax[hh])
            alpha = jnp.exp2(m - m_new)
            p = jnp.exp2(sts[hh] - m_new)
            l = alpha * l + jnp.sum(p, axis=0, keepdims=True)
            acc = alpha * acc + jnp.dot(vt, p.astype(BF16), preferred_element_type=F32)
            new.append((m_new, l, acc))
        return tuple(new)

    def body(j, carry):
        stats, cmax = carry
        nxt = col_max(scores(jnp.minimum(j + 1, qi - 1)))
        return fold(stats, scores(j), cmax, j), nxt

    assert tk == tq
    one = (jnp.full((1, tq), -jnp.inf, F32), jnp.zeros((1, tq), F32), jnp.zeros((V_HEAD, tq), F32))
    key = lax.broadcasted_iota(jnp.int32, (tk, tq), 0)
    qry = lax.broadcasted_iota(jnp.int32, (tk, tq), 1)
    diag = tuple(jnp.where(key <= qry, st, neg) for st in scores(qi))
    stats = fold((one, one), diag, col_max(diag), qi)
    stats, _ = lax.fori_loop(0, qi, body, (stats, col_max(scores(0))))
    (_, l0, acc0), (_, l1, acc1) = stats
    o_ref[...] = jnp.concatenate([acc0 / l0, acc1 / l1], axis=0).T.astype(o_ref.dtype)


def _prompt_attention(q, k, vt, batch, seq, tq, tk):
    n = batch * seq
    nq = seq // tq
    pairs = MLA_HEADS // 2
    return pl.pallas_call(
        functools.partial(_flash_kernel, tq=tq, tk=tk),
        grid=(batch, pairs, nq),
        in_specs=[pl.BlockSpec((tq, 2 * LANES), lambda b, p, i: (b * nq + i, p)),
                  pl.BlockSpec((seq, 2 * LANES), lambda b, p, i: (b, p)),
                  pl.BlockSpec((None, 2 * V_HEAD, seq), lambda b, p, i: (b, p, 0))],
        out_specs=pl.BlockSpec((tq, 2 * V_HEAD), lambda b, p, i: (b * nq + i, p)),
        out_shape=jax.ShapeDtypeStruct((n, MLA_HEADS * V_HEAD), BF16),
        compiler_params=_cparams("parallel", "parallel", "arbitrary"),
        name="prompt_attention",
    )(q, k, vt)


def _paged_attn_kernel(pt_ref, qlat_ref, qpe_ref, ckvn_ref, kpen_ref, ckv_hbm, kpe_hbm, o_ref,
                       ckv_buf, kpe_buf, sem, m_sc, l_sc, acc_sc, *, pps, sub_pages, t_new, n_chunks):
    step = pl.program_id(0)
    n_steps = pl.num_programs(0)
    c = step % n_chunks
    n_slots = ckv_buf.shape[0]
    ahead = n_slots - 1
    slot = step % n_slots
    rows = t_new * MLA_HEADS

    def page_copies(s, sl):
        out = []
        for k in range(pps):
            page = pt_ref[s * pps + k]
            tok = pl.ds(k * PAGE_SIZE, PAGE_SIZE)
            out.append(pltpu.make_async_copy(ckv_hbm.at[page], ckv_buf.at[sl, tok, :], sem.at[sl]))
            out.append(pltpu.make_async_copy(kpe_hbm.at[page], kpe_buf.at[sl, :, tok], sem.at[sl]))
        return out

    for s0 in range(ahead):
        @pl.when(jnp.logical_and(step == 0, s0 < n_steps))
        def _(s0=s0):
            for cp in page_copies(s0, s0):
                cp.start()

    @pl.when(c == 0)
    def _():
        m_sc[...] = jnp.full(m_sc.shape, -jnp.inf, F32)
        l_sc[...] = jnp.zeros(l_sc.shape, F32)
        acc_sc[...] = jnp.zeros(acc_sc.shape, F32)

    for cp in page_copies(step, slot):
        cp.wait()

    qlat = qlat_ref[...]
    qpe = qpe_ref[...]
    cks, scores = [], []
    for j in range(pps // sub_pages):
        tok = pl.ds(j * sub_pages * PAGE_SIZE, sub_pages * PAGE_SIZE)
        ck = ckv_buf[slot, tok, :].astype(BF16)
        kp = kpe_buf[slot, :, tok].astype(BF16)
        cks.append(ck)
        scores.append(_nt_dot(qlat, ck) + jnp.dot(qpe, kp, preferred_element_type=F32))
    m_run, l_run, acc = m_sc[...], l_sc[...], acc_sc[...]
    for ck, s in zip(cks, scores):
        m_new = jnp.maximum(m_run, jnp.max(s, axis=-1, keepdims=True))
        alpha = jnp.exp2(m_run - m_new)
        p = jnp.exp2(s - m_new)
        l_run = alpha * l_run + jnp.sum(p, axis=-1, keepdims=True)
        acc = alpha * acc + jnp.dot(p.astype(BF16), ck, preferred_element_type=F32)
        m_run = m_new
    nxt = jnp.where(step + ahead < n_steps, step + ahead, 0)
    nxt_slot = (step + ahead) % n_slots
    for cp in page_copies(nxt, nxt_slot):
        cp.start()
    m_sc[...], l_sc[...], acc_sc[...] = m_run, l_run, acc

    @pl.when(c == n_chunks - 1)
    def _():
        qlf = qlat.astype(F32)
        qpf = qpe.astype(F32)
        cn = ckvn_ref[...].astype(BF16).astype(F32)
        kn = kpen_ref[...].astype(BF16).astype(F32)
        tok = lax.broadcasted_iota(jnp.int32, (rows, 1), 0) // MLA_HEADS
        neg = jnp.finfo(F32).min
        s_new = []
        for j in range(t_new):
            sj = (jnp.sum(qlf * cn[j:j + 1, :], axis=-1, keepdims=True)
                  + jnp.sum(qpf * kn[j:j + 1, :], axis=-1, keepdims=True))
            s_new.append(jnp.where(tok >= j, sj, neg))
        m1 = m_sc[...]
        m2 = m1
        for sj in s_new:
            m2 = jnp.maximum(m2, sj)
        a2 = jnp.exp2(m1 - m2)
        l2 = a2 * l_sc[...]
        acc2 = a2 * acc_sc[...]
        for j, sj in enumerate(s_new):
            pj = jnp.exp2(sj - m2)
            l2 = l2 + pj
            acc2 = acc2 + pj.astype(BF16).astype(F32) * cn[j:j + 1, :]
        o_ref[...] = (acc2 / l2).astype(o_ref.dtype)

    for d in range(ahead):
        @pl.when(jnp.logical_and(step == n_steps - 1, n_steps - 1 - d >= 0))
        def _(d=d):
            for cp in page_copies(0, (n_steps - 1 - d + ahead) % n_slots):
                cp.wait()


def _paged_attention(page_table, qlat, qpe, ckv_new, kpe_new, cache_ckv, cache_kpe, t_new, pps):
    dec_batch, n_pages = page_table.shape
    rows = t_new * MLA_HEADS
    n_chunks = n_pages // pps
    by_seq = lambda s, pt: (s // n_chunks, 0)
    by_seq3 = lambda s, pt: (s // n_chunks, 0, 0)
    grid_spec = pltpu.PrefetchScalarGridSpec(
        num_scalar_prefetch=1,
        grid=(dec_batch * n_chunks,),
        in_specs=[pl.BlockSpec((rows, KV_LORA), by_seq),
                  pl.BlockSpec((rows, QK_ROPE), by_seq),
                  pl.BlockSpec((None, t_new, KV_LORA), by_seq3),
                  pl.BlockSpec((None, t_new, QK_ROPE), by_seq3),
                  pl.BlockSpec(memory_space=pl.ANY),
                  pl.BlockSpec(memory_space=pl.ANY)],
        out_specs=pl.BlockSpec((rows, KV_LORA), by_seq),
        scratch_shapes=[pltpu.VMEM((PAGE_RING_SLOTS, pps * PAGE_SIZE, KV_LORA), F32),
                        pltpu.VMEM((PAGE_RING_SLOTS, QK_ROPE, pps * PAGE_SIZE), F32),
                        pltpu.SemaphoreType.DMA((PAGE_RING_SLOTS,)),
                        pltpu.VMEM((rows, 1), F32), pltpu.VMEM((rows, 1), F32),
                        pltpu.VMEM((rows, KV_LORA), F32)],
    )
    return pl.pallas_call(
        functools.partial(_paged_attn_kernel, pps=pps, sub_pages=_tile_rows(pps, SOFTMAX_SUB_PAGES), t_new=t_new,
                          n_chunks=n_chunks),
        grid_spec=grid_spec,
        out_shape=jax.ShapeDtypeStruct((dec_batch * rows, KV_LORA), BF16),
        compiler_params=_cparams("arbitrary"),
        name="paged_attention",
    )(page_table.reshape(-1), qlat, qpe, ckv_new, kpe_new, cache_ckv, cache_kpe)


def _uv_proj_kernel(olat_ref, wuv_ref, o_ref):
    for h in range(MLA_HEADS):
        o_ref[:, h * V_HEAD:(h + 1) * V_HEAD] = jnp.dot(
            olat_ref[:, h * KV_LORA:(h + 1) * KV_LORA], wuv_ref[h],
            preferred_element_type=F32).astype(o_ref.dtype)


def _uv_proj(olat, wuv, tm):
    n = olat.shape[0]
    return pl.pallas_call(
        _uv_proj_kernel,
        grid=(n // tm,),
        in_specs=[pl.BlockSpec((tm, MLA_HEADS * KV_LORA), lambda i: (i, 0)), _const_spec(wuv.shape)],
        out_specs=pl.BlockSpec((tm, MLA_HEADS * V_HEAD), lambda i: (i, 0)),
        out_shape=jax.ShapeDtypeStruct((n, MLA_HEADS * V_HEAD), BF16),
        compiler_params=_cparams("parallel"),
        name="uv_proj",
    )(olat, wuv)


def _split3(x):
    hi = x.astype(BF16)
    r = x - hi.astype(F32)
    mid = r.astype(BF16)
    lo = (r - mid.astype(F32)).astype(BF16)
    return hi, mid, lo


def _expand_heads(v, expand_ref):
    hi = v.astype(BF16)
    mid = (v - hi.astype(F32)).astype(BF16)
    return jnp.dot(jnp.concatenate([hi, mid], axis=1), expand_ref[...], preferred_element_type=F32)


def _ssd_heads_wide(xc, dt, a_cum, a_cum_t, exp_cum, dte, tril, dskip_ref, expand_ref, y_ref, state_t):
    chunk = xc.shape[0]
    gw = HEADS_PER_GROUP * SSM_HEADDIM
    b_off = D_INNER
    c_off = D_INNER + SSM_GROUPS * SSM_STATE
    xs = xc[:, :D_INNER]
    dt_e = _expand_heads(dt, expand_ref)
    exp_cum_e = _expand_heads(exp_cum, expand_ref)
    dte_e = _expand_heads(dte, expand_ref)
    xdt = xs * dt_e
    xdt_b = xdt.astype(BF16)
    xdte_b = (xdt * dte_e).astype(BF16)
    skip = xs * dskip_ref[...]
    lane = lax.broadcasted_iota(jnp.int32, (chunk, LANES), 1)
    for g in range(SSM_GROUPS):
        gcols = slice(g * gw, (g + 1) * gw)
        bg = xc[:, b_off + g * SSM_STATE:b_off + (g + 1) * SSM_STATE].astype(BF16)
        cg = xc[:, c_off + g * SSM_STATE:c_off + (g + 1) * SSM_STATE].astype(BF16)
        cb = _nt_dot(cg, bg)
        st_g = state_t[:, gcols]
        y_off = jnp.dot(cg, st_g.astype(BF16), preferred_element_type=F32)
        for pair in range(HEADS_PER_GROUP // 2):
            h0 = g * HEADS_PER_GROUP + 2 * pair
            cols = slice(h0 * SSM_HEADDIM, (h0 + 2) * SSM_HEADDIM)
            rhs = xdt_b[:, cols]
            ys = []
            for h in (h0, h0 + 1):
                seg = a_cum[:, h:h + 1] - a_cum_t[h:h + 1, :]
                decay = jnp.exp(jnp.where(tril, seg, -jnp.inf))
                ys.append(jnp.dot((cb * decay).astype(BF16), rhs, preferred_element_type=F32))
            y = jnp.where(lane < SSM_HEADDIM, ys[0], ys[1])
            y_ref[:, cols] = (y + y_off[:, pair * LANES:(pair + 1) * LANES] * exp_cum_e[:, cols]
                              + skip[:, cols])
        cs_t = lax.dot_general(bg, xdte_b[:, gcols], (((0,), (0,)), ((), ())),
                               preferred_element_type=F32)
        state_t[:, gcols] = st_g * exp_cum_e[chunk - 1:chunk, gcols] + cs_t


def _ssd_kernel(*refs, l_in, chunk, has_init):
    if has_init:
        (xbc_ref, dt_ref, conv0_ref, ssm0_ref, cw_ref, cb_ref, dtb_ref, alog_ref, dskip_ref,
         y_ref, ssm_ref, conv_ref, xpad, state, dtpad) = refs
    else:
        (xbc_ref, dt_ref, cw_ref, cb_ref, dtb_ref, alog_ref, dskip_ref, expand_ref,
         y_ref, ssm_ref, conv_ref, xpad, state, dtpad) = refs
    c = pl.program_id(1)
    last = pl.num_programs(1) - 1
    tail = SUBLANES

    @pl.when(c == 0)
    def _():
        if has_init:
            xpad[0:tail, :] = conv0_ref[...]
            state[...] = ssm0_ref[...]
        else:
            xpad[0:tail, :] = jnp.zeros((tail, CONV_DIM), F32)
            state[...] = jnp.zeros(state.shape, F32)
        if l_in < chunk:
            xpad[tail:, :] = jnp.zeros((chunk, CONV_DIM), F32)

    xpad[tail:tail + l_in, :] = xbc_ref[...]

    conv = cb_ref[...] + cw_ref[CONV_WIDTH - 1:CONV_WIDTH, :] * xpad[tail:tail + chunk, :]
    for k in range(CONV_WIDTH - 1):
        off = tail - (CONV_WIDTH - 1) + k
        conv = conv + cw_ref[k:k + 1, :] * xpad[off:off + chunk, :]
    xc = _silu(conv)

    win = tail + ((l_in - (CONV_WIDTH - 1)) // SUBLANES) * SUBLANES
    sub = (l_in - (CONV_WIDTH - 1)) % SUBLANES

    @pl.when(c == last)
    def _():
        conv_ref[...] = xpad[win:win + SUBLANES, :][sub:sub + CONV_WIDTH - 1, :]

    xpad[0:tail, :] = xpad[chunk:chunk + tail, :]

    rows = lax.broadcasted_iota(jnp.int32, (chunk, LANES), 0)
    if l_in < chunk:
        dtpad[...] = jnp.zeros((chunk, LANES), F32)
        dtpad[0:l_in, :] = dt_ref[...]
        dt_raw = dtpad[...]
    else:
        dt_raw = dt_ref[...]
    dt = jnp.where(rows < l_in, _softplus(dt_raw + dtb_ref[...]), 0.0)
    a = -jnp.exp(alog_ref[...])
    da = dt * a

    ti = lax.broadcasted_iota(jnp.int32, (chunk, chunk), 0)
    si = lax.broadcasted_iota(jnp.int32, (chunk, chunk), 1)
    tril = ti >= si
    tril_b = tril.astype(BF16)
    hi, mid, lo = _split3(da)
    cum3 = jnp.dot(tril_b, jnp.concatenate([hi, mid, lo], axis=1), preferred_element_type=F32)
    a_cum = (cum3[:, 0:LANES] + cum3[:, LANES:2 * LANES]) + cum3[:, 2 * LANES:3 * LANES]
    a_cum_t = a_cum.T
    exp_cum = jnp.exp(a_cum)
    a_last = a_cum[chunk - 1:chunk, :]
    dte = jnp.exp(a_last - a_cum)

    gw = HEADS_PER_GROUP * SSM_HEADDIM
    b_off = D_INNER
    c_off = D_INNER + SSM_GROUPS * SSM_STATE

    if not has_init:
        assert l_in == chunk
        _ssd_heads_wide(xc, dt, a_cum, a_cum_t, exp_cum, dte, tril, dskip_ref, expand_ref, y_ref, state)

        @pl.when(c == last)
        def _():
            ssm_ref[...] = state[...].T
        return

    chunk_decay_t = jnp.exp(a_cum_t[:, chunk - 1:chunk])
    for g in range(SSM_GROUPS):
        bg = xc[:, b_off + g * SSM_STATE:b_off + (g + 1) * SSM_STATE].astype(BF16)
        cg = xc[:, c_off + g * SSM_STATE:c_off + (g + 1) * SSM_STATE].astype(BF16)
        cb = _nt_dot(cg, bg)
        st_g = state[g * gw:(g + 1) * gw, :]
        y_off = _nt_dot(cg, st_g.astype(BF16))
        xdte_parts = []
        for hl in range(HEADS_PER_GROUP):
            h = g * HEADS_PER_GROUP + hl
            xs = xc[:, h * SSM_HEADDIM:(h + 1) * SSM_HEADDIM]
            xdt = xs * dt[:, h:h + 1]
            seg = a_cum[:, h:h + 1] - a_cum_t[h:h + 1, :]
            decay = jnp.exp(jnp.where(tril, seg, -jnp.inf))
            y = jnp.dot((cb * decay).astype(BF16), xdt.astype(BF16), preferred_element_type=F32)
            y = y + y_off[:, hl * SSM_HEADDIM:(hl + 1) * SSM_HEADDIM] * exp_cum[:, h:h + 1]
            y = y + xs * dskip_ref[:, h * SSM_HEADDIM:(h + 1) * SSM_HEADDIM]
            y_ref[:, h * SSM_HEADDIM:(h + 1) * SSM_HEADDIM] = y[0:l_in, :]
            xdte_parts.append(xdt * dte[:, h:h + 1])
        xdte_t = jnp.concatenate(xdte_parts, axis=1).T.astype(BF16)
        cs = jnp.dot(xdte_t, bg, preferred_element_type=F32)
        for hl in range(HEADS_PER_GROUP):
            h = g * HEADS_PER_GROUP + hl
            r0 = g * gw + hl * SSM_HEADDIM
            state[r0:r0 + SSM_HEADDIM, :] = (st_g[hl * SSM_HEADDIM:(hl + 1) * SSM_HEADDIM, :]
                                             * chunk_decay_t[h:h + 1, :]
                                             + cs[hl * SSM_HEADDIM:(hl + 1) * SSM_HEADDIM, :])

    @pl.when(c == last)
    def _():
        ssm_ref[...] = state[...]


def _ssd(proj3, conv0, ssm0, cw, cb, dtb, alog, dskip, l_in, chunk):
    batch, seq, _ = proj3.shape
    nc = seq // l_in
    assert l_in == chunk or nc == 1
    has_init = conv0 is not None
    hp = SSM_HEADS * SSM_HEADDIM
    dt_blk = (PROJ_SMALL + SMALL_DT) // LANES
    in_specs = [pl.BlockSpec((None, l_in, CONV_DIM), lambda b, c: (b, c, PROJ_XBC // CONV_DIM)),
                pl.BlockSpec((None, l_in, LANES), lambda b, c: (b, c, dt_blk))]
    args = [proj3, proj3]
    if has_init:
        in_specs += [pl.BlockSpec((None, SUBLANES, CONV_DIM), lambda b, c: (b, 0, 0)),
                     pl.BlockSpec((None, hp, SSM_STATE), lambda b, c: (b, 0, 0))]
        args += [conv0, ssm0]
    consts = [cw, cb, dtb, alog, dskip]
    if not has_init:
        sel = (jnp.arange(LANES)[:, None] == jnp.arange(hp)[None, :] // SSM_HEADDIM).astype(BF16)
        consts.append(jnp.concatenate([sel, sel], axis=0))
    state_shape = (hp, SSM_STATE) if has_init else (SSM_STATE, hp)
    in_specs += [_const_spec(x.shape) for x in consts]
    args += consts
    return pl.pallas_call(
        functools.partial(_ssd_kernel, l_in=l_in, chunk=chunk, has_init=has_init),
        grid=(batch, nc),
        in_specs=in_specs,
        out_specs=[pl.BlockSpec((None, l_in, D_INNER), lambda b, c: (b, c, 0)),
                   pl.BlockSpec((None, hp, SSM_STATE), lambda b, c: (b, 0, 0)),
                   pl.BlockSpec((None, CONV_WIDTH - 1, CONV_DIM), lambda b, c: (b, 0, 0))],
        out_shape=[jax.ShapeDtypeStruct((batch, seq, D_INNER), F32),
                   jax.ShapeDtypeStruct((batch, hp, SSM_STATE), F32),
                   jax.ShapeDtypeStruct((batch, CONV_WIDTH - 1, CONV_DIM), F32)],
        scratch_shapes=[pltpu.VMEM((SUBLANES + chunk, CONV_DIM), F32), pltpu.VMEM(state_shape, F32),
                        pltpu.VMEM((chunk, LANES), F32)],
        compiler_params=_cparams("parallel", "arbitrary"),
        name="conv_ssd",
    )(*args)


def _layer_norm(x, g, b):
    mu = jnp.mean(x, axis=-1, keepdims=True)
    xc = x - mu
    var = jnp.mean(xc * xc, axis=-1, keepdims=True)
    return xc * lax.rsqrt(var + 1e-5) * g + b


def _merge_kernel(x_ref, attn_ref, y_ref, z_ref, g_ref, bg_ref, ng_ref, wa_ref, wb_ref, wo_ref,
                  l1g_ref, l1b_ref, h_ref):
    gsz = D_INNER // SSM_GROUPS
    yg = y_ref[...] * _silu(z_ref[...])
    parts = []
    for g in range(SSM_GROUPS):
        parts.append(_rms(yg[:, g * gsz:(g + 1) * gsz], ng_ref[:, g * gsz:(g + 1) * gsz], 1e-5).astype(BF16))
    ygn = jnp.concatenate(parts, axis=1)
    br_a = jnp.dot(attn_ref[...], wa_ref[...], preferred_element_type=F32)
    br_b = jnp.dot(ygn, wb_ref[...], preferred_element_type=F32)
    gates = _sigmoid(g_ref[...] + bg_ref[...])
    merged = gates[:, :D_MODEL] * br_a + gates[:, D_MODEL:] * br_b
    mix = jnp.dot(merged.astype(BF16), wo_ref[...], preferred_element_type=F32)
    h_ref[...] = _layer_norm(ALPHA * x_ref[...] + mix, l1g_ref[...], l1b_ref[...])


def _merge(x, attn, y, proj, bg, ng, wa, wb, wo, l1g, l1b, tm):
    n = x.shape[0]
    row = lambda w: pl.BlockSpec((tm, w), lambda i: (i, 0))
    consts = [bg, ng, wa, wb, wo, l1g, l1b]
    return pl.pallas_call(
        _merge_kernel,
        grid=(n // tm,),
        in_specs=[row(D_MODEL), row(MLA_HEADS * V_HEAD), row(D_INNER),
                  pl.BlockSpec((tm, D_INNER), lambda i: (i, PROJ_Z // D_INNER)),
                  pl.BlockSpec((tm, 2 * D_MODEL), lambda i: (i, PROJ_G // (2 * D_MODEL)))]
                 + [_const_spec(c.shape) for c in consts],
        out_specs=row(D_MODEL),
        out_shape=jax.ShapeDtypeStruct((n, D_MODEL), F32),
        compiler_params=_cparams("parallel"),
        name="merge_ln1",
    )(x, attn, y, proj, proj, *consts)


def _ffn_kernel(h_ref, wg_ref, wu_ref, wd_ref, l2g_ref, l2b_ref, o_ref):
    h = h_ref[...]
    hb = h.astype(BF16)
    g = jnp.dot(hb, wg_ref[...], preferred_element_type=F32)
    u = jnp.dot(hb, wu_ref[...], preferred_element_type=F32)
    f = jnp.dot((_silu(g) * u).astype(BF16), wd_ref[...], preferred_element_type=F32)
    o_ref[...] = _layer_norm(ALPHA * h + f, l2g_ref[...], l2b_ref[...])


def _ffn(h, wg, wu, wd, l2g, l2b, tm):
    n = h.shape[0]
    row = pl.BlockSpec((tm, D_MODEL), lambda i: (i, 0))
    consts = [wg, wu, wd, l2g, l2b]
    return pl.pallas_call(
        _ffn_kernel,
        grid=(n // tm,),
        in_specs=[row] + [_const_spec(c.shape) for c in consts],
        out_specs=row,
        out_shape=jax.ShapeDtypeStruct((n, D_MODEL), F32),
        compiler_params=_cparams("parallel"),
        name="ffn_ln2",
    )(h, *consts)


def _rot_half(w):
    half = QK_ROPE // 2
    return jnp.concatenate([-w[..., half:], w[..., :half]], axis=-1)


def _prep_weights(w_in, w_uq, w_ukv):
    wq, wkv, wkpe, wz, wxbc, wdt, wg = jnp.split(w_in.T.astype(BF16), IN_OFFSETS, axis=0)
    zr = lambda n: jnp.zeros((n, D_MODEL), BF16)
    pad_hi = LANES - PE_LANE0 - QK_ROPE
    half = QK_ROPE // 2
    krot = jnp.concatenate([-wkpe[half:], wkpe[:half]], axis=0)
    small = jnp.concatenate([wq, wkv, zr(PE_LANE0), wkpe, zr(pad_hi), zr(PE_LANE0), krot, zr(pad_hi),
                             wdt, zr(LANES - SSM_HEADS)], axis=0)
    w_in_p = jnp.concatenate([wxbc, small, wz, wg], axis=0)

    uq = w_uq.reshape(Q_LORA, MLA_HEADS, QK_NOPE + QK_ROPE)
    nope, pe = uq[..., :QK_NOPE], uq[..., QK_NOPE:]
    rot = _rot_half(pe)
    zq = lambda n: jnp.zeros((Q_LORA, MLA_HEADS, n), F32)
    hw = MLA_HEADS * LANES
    wq_a = jnp.concatenate([nope, pe, zq(pad_hi)], axis=-1).reshape(Q_LORA, hw)
    wq_b = jnp.concatenate([zq(PE_LANE0), rot, zq(pad_hi)], axis=-1).reshape(Q_LORA, hw)
    wq_prompt = jnp.concatenate([wq_a, wq_b], axis=1).astype(BF16)
    wq_n = jnp.concatenate([nope, zq(LANES - QK_NOPE)], axis=-1).reshape(Q_LORA, hw)
    wq_sample = jnp.concatenate([wq_n, pe.reshape(Q_LORA, -1), rot.reshape(Q_LORA, -1)], axis=1).astype(BF16)

    ukv = w_ukv.reshape(KV_LORA, MLA_HEADS, QK_NOPE + V_HEAD)
    k_nope, v = ukv[..., :QK_NOPE], ukv[..., QK_NOPE:]
    zk = jnp.zeros((KV_LORA, MLA_HEADS, LANES - QK_NOPE), F32)
    wkv_prompt = jnp.concatenate([jnp.concatenate([k_nope, zk], axis=-1).reshape(KV_LORA, hw),
                                  v.reshape(KV_LORA, -1)], axis=1).astype(BF16)
    wuk_t = jnp.concatenate([k_nope, zk], axis=-1).transpose(1, 2, 0).astype(BF16)
    wuv = v.transpose(1, 0, 2).astype(BF16)
    return w_in_p, wq_prompt, wq_sample, wkv_prompt, wuk_t, wuv


def _rope_tables(pos):
    half = QK_ROPE // 2
    inv = ROPE_THETA ** (-jnp.arange(0, QK_ROPE, 2, dtype=F32) / QK_ROPE)
    ang = pos.astype(F32)[:, None] * inv[None, :]
    c2 = jnp.concatenate([jnp.cos(ang), jnp.cos(ang)], axis=1)
    s2 = jnp.concatenate([jnp.sin(ang), jnp.sin(ang)], axis=1)
    t = pos.shape[0]
    cos = jnp.concatenate([jnp.ones((t, PE_LANE0), F32), c2, jnp.zeros((t, LANES - PE_LANE0 - QK_ROPE), F32)], axis=1)
    sin = jnp.concatenate([jnp.zeros((t, PE_LANE0), F32), s2, jnp.zeros((t, LANES - PE_LANE0 - QK_ROPE), F32)], axis=1)
    return cos, sin, jnp.tile(c2, (1, MLA_HEADS)), jnp.tile(s2, (1, MLA_HEADS))


def _tile_rows(n, pref):
    t = min(pref, n)
    while n % t:
        t //= 2
    return t


def kernel(x_prompt, x_sample, cache_ckv, cache_kpe, state_ssm, state_conv, page_table, w_in, b_gate, q_norm_g, kv_norm_g, w_uq, w_ukv, conv_w, conv_b, dt_bias, a_log, d_skip, ssm_norm_g, w_branch_a, w_branch_b, w_out, ln1_g, ln1_b, w_ffn_gate, w_ffn_up, w_ffn_down, ln2_g, ln2_b):
    assert w_in.shape[0] == DEPTH == 1
    batch, seq, _ = x_prompt.shape
    dec_batch, t_new, _ = x_sample.shape
    n_pages = page_table.shape[1]
    past_len = n_pages * PAGE_SIZE
    n_p = batch * seq
    n_s = dec_batch * t_new

    w_in_p, wq_prompt, wq_sample, wkv_prompt, wuk_t, wuv = _prep_weights(w_in[0], w_uq[0], w_ukv[0])
    row2 = lambda v: v.reshape(1, -1).astype(F32)
    gq, gkv = row2(q_norm_g[0]), row2(kv_norm_g[0])
    pad_heads = lambda v: jnp.concatenate([v.astype(F32), jnp.zeros((LANES - SSM_HEADS,), F32)]).reshape(1, LANES)
    dtb, alog = pad_heads(dt_bias[0]), pad_heads(a_log[0])
    dskip = jnp.repeat(d_skip[0].astype(F32), SSM_HEADDIM).reshape(1, D_INNER)
    cw, cb = conv_w[0].astype(F32), row2(conv_b[0])
    bg, ng = row2(b_gate[0]), row2(ssm_norm_g[0])
    wa, wb, wo = w_branch_a[0].astype(BF16), w_branch_b[0].astype(BF16), w_out[0].astype(BF16)
    wfg, wfu, wfd = w_ffn_gate[0].astype(BF16), w_ffn_up[0].astype(BF16), w_ffn_down[0].astype(BF16)
    l1g, l1b, l2g, l2b = row2(ln1_g[0]), row2(ln1_b[0]), row2(ln2_g[0]), row2(ln2_b[0])

    xp = x_prompt.reshape(n_p, D_MODEL)
    proj_p = _matmul(xp, w_in_p, _tile_rows(n_p, IN_PROJ_ROWS), IN_PROJ_COL_TILE, F32)
    cos_p, sin_p, _, _ = _rope_tables(jnp.arange(seq))
    q_p, k_p, vt_p, ckv_p, kpe_p = _mla_prep_prompt(proj_p, cos_p, sin_p, gq, gkv, wq_prompt, wkv_prompt,
                                                   _tile_rows(seq, MLA_PREP_ROWS))
    tq_p = _tile_rows(seq, ATTN_TILE)
    attn_p = _prompt_attention(q_p, k_p, vt_p, batch, seq, tq_p, tq_p)
    chunk_p = min(SSD_CHUNK, seq)
    y_p, ssm_p, conv_p = _ssd(proj_p.reshape(batch, seq, PROJ_COLS), None, None, cw, cb, dtb, alog, dskip,
                              chunk_p, chunk_p)
    h_p = _merge(xp, attn_p, y_p.reshape(n_p, D_INNER), proj_p, bg, ng, wa, wb, wo, l1g, l1b,
                 _tile_rows(n_p, DENSE_ROWS))
    out_p = _ffn(h_p, wfg, wfu, wfd, l2g, l2b, _tile_rows(n_p, DENSE_ROWS))

    xs = x_sample.reshape(n_s, D_MODEL)
    proj_s = _matmul(xs, w_in_p, _tile_rows(n_s, SAMPLE_PROJ_ROWS), IN_PROJ_COL_TILE, F32)
    cos_s, sin_s, cosp_s, sinp_s = _rope_tables(past_len + jnp.arange(t_new))
    tm_s = t_new * _tile_rows(dec_batch, SAMPLE_PREP_SEQS)
    rep = lambda t: jnp.tile(t, (tm_s // t_new, 1))
    qlat_s, qpe_s, ckv_s, kpe_s = _mla_prep_sample(proj_s, rep(cos_s), rep(sin_s), rep(cosp_s), rep(sinp_s),
                                                   gq, gkv, wq_sample, wuk_t, tm_s)
    pps = _tile_rows(n_pages, PAGES_PER_STEP)
    kpe_t = jnp.swapaxes(cache_kpe.reshape(cache_kpe.shape[1:]), 1, 2)
    olat_s = _paged_attention(page_table, qlat_s.reshape(n_s * MLA_HEADS, KV_LORA),
                              qpe_s.reshape(n_s * MLA_HEADS, QK_ROPE),
                              ckv_s.reshape(dec_batch, t_new, KV_LORA), kpe_s.reshape(dec_batch, t_new, QK_ROPE),
                              cache_ckv.reshape(cache_ckv.shape[1:]), kpe_t, t_new, pps)
    attn_s = _uv_proj(olat_s.reshape(n_s, MLA_HEADS * KV_LORA), wuv, _tile_rows(n_s, DENSE_ROWS))
    conv0 = jnp.concatenate([jnp.zeros((dec_batch, SUBLANES - (CONV_WIDTH - 1), CONV_DIM), F32),
                             state_conv[0].astype(F32)], axis=1)
    ssm0 = state_ssm.astype(F32).reshape(dec_batch, SSM_HEADS * SSM_HEADDIM, SSM_STATE)
    y_s, ssm_s, conv_s = _ssd(proj_s.reshape(dec_batch, t_new, PROJ_COLS), conv0, ssm0, cw, cb, dtb, alog, dskip,
                              t_new, SUBLANES * pl.cdiv(t_new, SUBLANES))
    h_s = _merge(xs, attn_s, y_s.reshape(n_s, D_INNER), proj_s, bg, ng, wa, wb, wo, l1g, l1b,
                 _tile_rows(n_s, DENSE_ROWS))
    out_s = _ffn(h_s, wfg, wfu, wfd, l2g, l2b, _tile_rows(n_s, DENSE_ROWS))

    st_shape = (SSM_HEADS, SSM_HEADDIM, SSM_STATE)
    return (out_p.reshape(batch, seq, D_MODEL), out_s.reshape(dec_batch, t_new, D_MODEL),
            ckv_p.reshape(1, batch, seq, KV_LORA), kpe_p.reshape(1, batch, seq, QK_ROPE),
            ssm_p.reshape(1, batch, *st_shape), conv_p.reshape(1, batch, CONV_WIDTH - 1, CONV_DIM),
            ckv_s.reshape(1, dec_batch, t_new, KV_LORA), kpe_s.reshape(1, dec_batch, t_new, QK_ROPE),
            ssm_s.reshape(1, dec_batch, *st_shape), conv_s.reshape(1, dec_batch, CONV_WIDTH - 1, CONV_DIM))
```
